```python
import math
import jax, jax.numpy as jnp
from jax import lax
import numpy as np

D_MODEL = 1024
BATCH = 2
SEQ = 8192
DEPTH = 4
DEC_BATCH = 128
DEC_SEQ = 4
PAST_LEN = 8192
PAGE_SIZE = 128

N_META = 16
N_EVEN = (DEPTH + 1) // 2
N_ODD = DEPTH // 2
EXPAND = 2
D_INNER = EXPAND * D_MODEL
HGRN_WIDTH = D_INNER // 2
HGRN_DK = 128
HGRN_HEADS = HGRN_WIDTH // HGRN_DK
HGRN_DV = HGRN_WIDTH // HGRN_HEADS
MLA_WIDTH = D_INNER - HGRN_WIDTH
MLA_V = 128
MLA_HEADS = MLA_WIDTH // MLA_V
MLA_NOPE = 128
MLA_ROPE = 64
Q_LORA = 512
KV_LORA = 256
ROPE_THETA = 10000.0
MLA_SCALE = (MLA_NOPE + MLA_ROPE) ** -0.5
Q_BLOCK = 128
MLSTM_HEADS = 4
MLSTM_DH = D_INNER // MLSTM_HEADS
QKV_BLOCK = 4
CONV_W = 4
CHUNK = 64
EV_SPLITS = (HGRN_HEADS * HGRN_DK, HGRN_HEADS * HGRN_DK, HGRN_WIDTH, HGRN_WIDTH,
             Q_LORA, KV_LORA, MLA_ROPE, MLA_WIDTH)
EV_IN = sum(EV_SPLITS)
OD_IN = 2 * D_INNER
NEG_BIG = -1e30

kernel_name = "hgrn2_mla_mlstm_hybrid_step"

F32 = jnp.float32


def rms_norm(x, w, eps=1e-6):
    xf = x.astype(F32)
    y = xf * lax.rsqrt(jnp.mean(xf * xf, axis=-1, keepdims=True) + eps)
    return (y * w.astype(F32)).astype(x.dtype)


def head_layer_norm(x, w, eps=1e-5):
    xf = x.astype(F32)
    mu = jnp.mean(xf, axis=-1, keepdims=True)
    xc = xf - mu
    var = jnp.mean(xc * xc, axis=-1, keepdims=True)
    return xc * lax.rsqrt(var + eps) * w.astype(F32)


def rope(x, pos):
    half = x.shape[-1] // 2
    inv = ROPE_THETA ** (-jnp.arange(half, dtype=F32) / half)
    ang = pos.astype(F32)[:, None] * inv[None, :]
    shape = (1, pos.shape[0]) + (1,) * (x.ndim - 3) + (half,)
    cos = jnp.cos(ang).reshape(shape)
    sin = jnp.sin(ang).reshape(shape)
    xf = x.astype(F32)
    x1, x2 = xf[..., :half], xf[..., half:]
    return jnp.concatenate([x1 * cos - x2 * sin, x1 * sin + x2 * cos], axis=-1).astype(x.dtype)


def causal_conv(x, buf, w, b):
    xp = jnp.concatenate([buf.astype(x.dtype), x], axis=1)
    T = x.shape[1]
    y = b + sum(xp[:, i:i + T] * w[i] for i in range(CONV_W))
    return y, xp[:, xp.shape[1] - (CONV_W - 1):]


def blockdiag(x, w):
    B, T, Dd = x.shape
    return jnp.einsum('btgi,gio->btgo', x.reshape(B, T, -1, QKV_BLOCK), w).reshape(B, T, Dd)


def hgrn_step(S, q, k, logf, v):
    q, k, logf, v = (a.astype(F32) for a in (q, k, logf, v))
    L = q.shape[1]
    G = jnp.cumsum(logf, axis=1)
    mask = jnp.tril(jnp.ones((L, L), bool))[None, :, :, None, None]
    decay = jnp.exp(jnp.where(mask, G[:, :, None] - G[:, None], NEG_BIG))
    A = jnp.sum(q[:, :, None] * k[:, None] * decay, axis=-1)
    o = jnp.einsum('btjh,bjhv->bthv', A, v) + jnp.einsum('bthk,bhkv->bthv', q * jnp.exp(G), S)
    GL = G[:, -1]
    S_new = jnp.exp(GL)[..., None] * S + jnp.einsum('bjhk,bjhv->bhkv', k * jnp.exp(GL[:, None] - G), v)
    return S_new, o


def mlstm_step(state, q, k, v, logi, logf):
    C, n, m = state
    q, k, v, logi, logf = (a.astype(F32) for a in (q, k, v, logi, logf))
    L = q.shape[1]
    b = jnp.cumsum(logf, axis=1)
    mask = jnp.tril(jnp.ones((L, L), bool))[None, :, :, None]
    D = jnp.where(mask, b[:, :, None] - b[:, None] + logi[:, None], NEG_BIG)
    inter = b + m[:, None]
    m_t = jnp.maximum(inter, jnp.max(D, axis=2))
    w_inter = jnp.exp(inter - m_t)
    s = jnp.einsum('bthk,bjhk->btjh', q, k) * jnp.exp(D - m_t[:, :, None])
    num = jnp.einsum('btjh,bjhv->bthv', s, v) + w_inter[..., None] * jnp.einsum('bthk,bhkv->bthv', q, C)
    den = jnp.sum(s, axis=2) + w_inter * jnp.einsum('bthk,bhk->bth', q, n)
    h = num / jnp.maximum(jnp.abs(den), jnp.exp(-m_t))[..., None]
    bL = b[:, -1]
    m_new = m_t[:, -1]
    carry = jnp.exp(bL + m - m_new)
    wj = jnp.exp(bL[:, None] - b + logi - m_new[:, None])
    kw = k * wj[..., None]
    C_new = carry[..., None, None] * C + jnp.einsum('bjhk,bjhv->bhkv', kw, v)
    n_new = carry[..., None] * n + jnp.sum(kw, axis=1)
    return (C_new, n_new, m_new), h


def run_prompt(step, state0, seqs, chunk):
    state, out_lead = step(state0, *(s[:, :N_META] for s in seqs))
    rest = [s[:, N_META:] for s in seqs]
    B, T = rest[0].shape[:2]
    nc = T // chunk
    xs = tuple(jnp.moveaxis(s.reshape(B, nc, chunk, *s.shape[2:]), 1, 0) for s in rest)
    state, outs = lax.scan(lambda st, xc: step(st, *xc), state, xs)
    outs = jnp.moveaxis(outs, 0, 1).reshape(B, T, *outs.shape[3:])
    return state, jnp.concatenate([out_lead, outs], axis=1)


def mla_attend_prompt(q_lat, q_rope, ckv, krope):
    B, T, H, C = q_lat.shape
    nb = -(-T // Q_BLOCK)
    pad = nb * Q_BLOCK - T

    def blocks(a):
        a = jnp.pad(a, ((0, 0), (0, pad), (0, 0), (0, 0)))
        return jnp.moveaxis(a.reshape(B, nb, Q_BLOCK, *a.shape[2:]), 1, 0)

    kpos = jnp.arange(T)

    def one(args):
        ql, qr, start = args
        qpos = start + jnp.arange(Q_BLOCK)
        s = (jnp.einsum('bqhc,bkc->bhqk', ql, ckv) + jnp.einsum('bqhr,bkr->bhqk', qr, krope)).astype(F32)
        s = jnp.where(kpos[None, :] <= qpos[:, None], s * MLA_SCALE, NEG_BIG)
        p = jax.nn.softmax(s, axis=-1).astype(ckv.dtype)
        return jnp.einsum('bhqk,bkc->bqhc', p, ckv)

    o = lax.map(one, (blocks(q_lat), blocks(q_rope), jnp.arange(nb) * Q_BLOCK))
    return jnp.moveaxis(o, 0, 1).reshape(B, nb * Q_BLOCK, H, C)[:, :T]


def mla_attend_sample(q_lat, q_rope, ckv_new, kr_new, ckv_pool, kr_pool, page_table):
    Bd, L = q_lat.shape[:2]
    ckv_past = ckv_pool[page_table].reshape(Bd, -1, ckv_pool.shape[-1]).astype(ckv_new.dtype)
    kr_past = kr_pool[page_table].reshape(Bd, -1, kr_pool.shape[-1]).astype(kr_new.dtype)
    P = ckv_past.shape[1]
    s_past = (jnp.einsum('bqhc,bkc->bhqk', q_lat, ckv_past)
              + jnp.einsum('bqhr,bkr->bhqk', q_rope, kr_past)).astype(F32) * MLA_SCALE
    s_new = (jnp.einsum('bqhc,bkc->bhqk', q_lat, ckv_new)
             + jnp.einsum('bqhr,bkr->bhqk', q_rope, kr_new)).astype(F32)
    s_new = jnp.where(jnp.tril(jnp.ones((L, L), bool)), s_new * MLA_SCALE, NEG_BIG)
    p = jax.nn.softmax(jnp.concatenate([s_past, s_new], axis=-1), axis=-1).astype(ckv_new.dtype)
    return (jnp.einsum('bhqk,bkc->bqhc', p[..., :P], ckv_past)
            + jnp.einsum('bhqk,bkc->bqhc', p[..., P:], ckv_new))


def even_project(xn, pos, lb, w_in, q_norm_w, w_q_b, kv_norm_w, w_uk):
    B, T = xn.shape[:2]
    u = xn @ w_in
    idx = np.cumsum(EV_SPLITS)[:-1].tolist()
    q_a, f_a, i_a, g_a, cq, ckv, kpe, g_b = jnp.split(u, idx, axis=-1)
    q_h = jax.nn.silu(q_a).reshape(B, T, HGRN_HEADS, HGRN_DK)
    lbf = lb.astype(F32)
    fa = f_a.astype(F32)
    f = lbf + (1.0 - lbf) * jax.nn.sigmoid(fa)
    logf = jnp.log(f).reshape(B, T, HGRN_HEADS, HGRN_DK)
    k_h = ((1.0 - lbf) * jax.nn.sigmoid(-fa)).reshape(B, T, HGRN_HEADS, HGRN_DK)
    v_h = i_a.reshape(B, T, HGRN_HEADS, HGRN_DV)
    qb = (rms_norm(cq, q_norm_w) @ w_q_b).reshape(B, T, MLA_HEADS, MLA_NOPE + MLA_ROPE)
    q_nope, q_pe = qb[..., :MLA_NOPE], qb[..., MLA_NOPE:]
    q_lat = jnp.einsum('bthn,chn->bthc', q_nope, w_uk)
    q_rope = rope(q_pe, pos)
    c = rms_norm(ckv, kv_norm_w)
    k_rope = rope(kpe, pos)
    return (q_h, k_h, logf, v_h), (q_lat, q_rope, c, k_rope), (g_a, g_b)


def even_output(o_a, o_b, g_a, g_b, hgrn_norm_w, w_uv, w_out, dtype):
    B, T = o_a.shape[:2]
    ya = rms_norm(o_a, hgrn_norm_w.reshape(HGRN_HEADS, HGRN_DV)).reshape(B, T, HGRN_WIDTH).astype(dtype)
    yb = jnp.einsum('bthc,chv->bthv', o_b, w_uv).reshape(B, T, MLA_WIDTH)
    y = jnp.concatenate([ya * jax.nn.silu(g_a), yb * jax.nn.silu(g_b)], axis=-1)
    return (y @ w_out).astype(dtype)


def odd_project(xn, conv_buf, w_in, conv_w, conv_b, w_q, w_k, w_v, w_gate, b_gate):
    B, T = xn.shape[:2]
    u = xn @ w_in
    x_in, z = u[..., :D_INNER], u[..., D_INNER:]
    xc, new_buf = causal_conv(x_in, conv_buf, conv_w, conv_b)
    xc = jax.nn.silu(xc)
    q = blockdiag(xc, w_q)
    k = blockdiag(xc, w_k)
    v = blockdiag(x_in, w_v)
    gates = (jnp.concatenate([q, k, v], axis=-1) @ w_gate + b_gate).astype(F32)
    logi = gates[..., :MLSTM_HEADS]
    logf = jax.nn.log_sigmoid(gates[..., MLSTM_HEADS:])
    hs = (B, T, MLSTM_HEADS, MLSTM_DH)
    rec = (q.reshape(hs), k.reshape(hs) * (MLSTM_DH ** -0.5), v.reshape(hs), logi, logf)
    return rec, xc, z, new_buf


def odd_output(h, xc, z, norm_w, skip, w_out, dtype):
    B, T = h.shape[:2]
    hn = head_layer_norm(h, norm_w.reshape(MLSTM_HEADS, MLSTM_DH)).reshape(B, T, D_INNER).astype(dtype)
    y = (hn + skip * xc) * jax.nn.silu(z)
    return (y @ w_out).astype(dtype)


def setup_inputs(seed: int = 0) -> dict:
    key = jax.random.key(seed)
    ks = iter(jax.random.split(key, 48))

    def nrm(shape, scale):
        return jax.random.normal(next(ks), shape, F32) * scale

    n_pages = PAST_LEN // PAGE_SIZE
    n_used = DEC_BATCH * n_pages
    n_pool = (n_used * 5 + 3) // 4
    perm = jax.random.permutation(next(ks), n_pool)
    page_table = perm[:n_used].reshape(DEC_BATCH, n_pages).astype(jnp.int32)
    NE, NO = N_EVEN, N_ODD
    b_gate = jnp.concatenate([nrm((NO, MLSTM_HEADS), 0.1),
                              jnp.linspace(3.0, 6.0, MLSTM_HEADS, dtype=F32)[None] + nrm((NO, MLSTM_HEADS), 0.1)], axis=-1)
    return {
        "x_prompt": nrm((BATCH, SEQ, D_MODEL), 1.0),
        "x_sample": nrm((DEC_BATCH, DEC_SEQ, D_MODEL), 1.0),
        "cache_mla_ckv": nrm((NE, n_pool, PAGE_SIZE, KV_LORA), 1.0),
        "cache_mla_krope": nrm((NE, n_pool, PAGE_SIZE, MLA_ROPE), 1.0),
        "state_hgrn": nrm((NE, DEC_BATCH, HGRN_HEADS, HGRN_DK, HGRN_DV), 0.5),
        "state_mlstm_C": nrm((NO, DEC_BATCH, MLSTM_HEADS, MLSTM_DH, MLSTM_DH), 0.05),
        "state_mlstm_n": nrm((NO, DEC_BATCH, MLSTM_HEADS, MLSTM_DH), 0.05),
        "state_mlstm_m": nrm((NO, DEC_BATCH, MLSTM_HEADS), 1.0),
        "state_mlstm_conv": nrm((NO, DEC_BATCH, CONV_W - 1, D_INNER), 1.0),
        "page_table": page_table,
        "meta_tokens": nrm((N_META, D_MODEL), 1.0),
        "ev_norm_w": 1.0 + nrm((NE, D_MODEL), 0.05),
        "ev_w_in": nrm((NE, D_MODEL, EV_IN), D_MODEL ** -0.5),
        "hgrn_lower_bounds": nrm((NE, HGRN_HEADS * HGRN_DK), 1.0),
        "hgrn_norm_w": 1.0 + nrm((NE, HGRN_WIDTH), 0.05),
        "mla_q_norm_w": 1.0 + nrm((NE, Q_LORA), 0.05),
        "mla_w_q_b": nrm((NE, Q_LORA, MLA_HEADS * (MLA_NOPE + MLA_ROPE)), Q_LORA ** -0.5),
        "mla_kv_norm_w": 1.0 + nrm((NE, KV_LORA), 0.05),
        "mla_w_uk": nrm((NE, KV_LORA, MLA_HEADS, MLA_NOPE), KV_LORA ** -0.5),
        "mla_w_uv": nrm((NE, KV_LORA, MLA_HEADS, MLA_V), KV_LORA ** -0.5),
        "ev_w_out": nrm((NE, D_INNER, D_MODEL), D_INNER ** -0.5),
        "od_norm_w": 1.0 + nrm((NO, D_MODEL), 0.05),
        "od_w_in": nrm((NO, D_MODEL, OD_IN), D_MODEL ** -0.5),
        "mlstm_conv_w": nrm((NO, CONV_W, D_INNER), CONV_W ** -0.5),
        "mlstm_conv_b": nrm((NO, D_INNER), 0.02),
        "mlstm_w_q": nrm((NO, D_INNER // QKV_BLOCK, QKV_BLOCK, QKV_BLOCK), QKV_BLOCK ** -0.5),
        "mlstm_w_k": nrm((NO, D_INNER // QKV_BLOCK, QKV_BLOCK, QKV_BLOCK), QKV_BLOCK ** -0.5),
        "mlstm_w_v": nrm((NO, D_INNER // QKV_BLOCK, QKV_BLOCK, QKV_BLOCK), QKV_BLOCK ** -0.5),
        "mlstm_w_gate": nrm((NO, 3 * D_INNER, 2 * MLSTM_HEADS), 0.1 * (3 * D_INNER) ** -0.5),
        "mlstm_b_gate": b_gate,
        "mlstm_norm_w": 1.0 + nrm((NO, D_INNER), 0.05),
        "mlstm_skip": 1.0 + nrm((NO, D_INNER), 0.05),
        "od_w_out": nrm((NO, D_INNER, D_MODEL), D_INNER ** -0.5),
        "final_norm_w": 1.0 + nrm((D_MODEL,), 0.05),
    }


def reference(x_prompt, x_sample, cache_mla_ckv, cache_mla_krope, state_hgrn, state_mlstm_C,
              state_mlstm_n, state_mlstm_m, state_mlstm_conv, page_table, meta_tokens,
              ev_norm_w, ev_w_in, hgrn_lower_bounds, hgrn_norm_w, mla_q_norm_w, mla_w_q_b,
              mla_kv_norm_w, mla_w_uk, mla_w_uv, ev_w_out, od_norm_w, od_w_in, mlstm_conv_w,
              mlstm_conv_b, mlstm_w_q, mlstm_w_k, mlstm_w_v, mlstm_w_gate, mlstm_b_gate,
              mlstm_norm_w, mlstm_skip, od_w_out, final_norm_w):
    dt = x_prompt.dtype
    B = x_prompt.shape[0]
    hp = jnp.concatenate([jnp.broadcast_to(meta_tokens.astype(dt)[None], (B, N_META, D_MODEL)), x_prompt], axis=1)
    hs = x_sample
    Bd, Ls = x_sample.shape[:2]
    T = hp.shape[1]
    pos_p = jnp.arange(T)
    past_len = page_table.shape[1] * cache_mla_ckv.shape[2]
    pos_s = past_len + jnp.arange(Ls)
    sm = jax.nn.softmax(hgrn_lower_bounds.astype(F32), axis=0)
    lbs = jnp.cumsum(sm, axis=0) - sm[0]

    ckv_p, kr_p, hg_p, mC_p, mn_p, mm_p, cv_p = [], [], [], [], [], [], []
    ckv_s, kr_s, hg_s, mC_s, mn_s, mm_s, cv_s = [], [], [], [], [], [], []
    for l in range(DEPTH):
        e = l // 2
        if l % 2 == 0:
            pw = (ev_w_in[e], mla_q_norm_w[e], mla_w_q_b[e], mla_kv_norm_w[e], mla_w_uk[e])
            ow = (hgrn_norm_w[e], mla_w_uv[e], ev_w_out[e])
            rec, att, (g_a, g_b) = even_project(rms_norm(hp, ev_norm_w[e]), pos_p, lbs[e], *pw)
            S0 = jnp.zeros((B, HGRN_HEADS, HGRN_DK, HGRN_DV), F32)
            S_p, o_a = run_prompt(hgrn_step, S0, rec, CHUNK)
            o_b = mla_attend_prompt(*att)
            hp = hp + even_output(o_a, o_b, g_a, g_b, *ow, dt)
            ckv_p.append(att[2]); kr_p.append(att[3]); hg_p.append(S_p)
            rec, att, (g_a, g_b) = even_project(rms_norm(hs, ev_norm_w[e]), pos_s, lbs[e], *pw)
            S_s, o_a = hgrn_step(state_hgrn[e].astype(F32), *rec)
            o_b = mla_attend_sample(*att, cache_mla_ckv[e], cache_mla_krope[e], page_table)
            hs = hs + even_output(o_a, o_b, g_a, g_b, *ow, dt)
            ckv_s.append(att[2]); kr_s.append(att[3]); hg_s.append(S_s)
        else:
            pw = (od_w_in[e], mlstm_conv_w[e], mlstm_conv_b[e], mlstm_w_q[e], mlstm_w_k[e],
                  mlstm_w_v[e], mlstm_w_gate[e], mlstm_b_gate[e])
            ow = (mlstm_norm_w[e], mlstm_skip[e], od_w_out[e])
            buf0 = jnp.zeros((B, CONV_W - 1, D_INNER), dt)
            rec, xc, z, buf_p = odd_project(rms_norm(hp, od_norm_w[e]), buf0, *pw)
            st0 = (jnp.zeros((B, MLSTM_HEADS, MLSTM_DH, MLSTM_DH), F32),
                   jnp.zeros((B, MLSTM_HEADS, MLSTM_DH), F32),
                   jnp.zeros((B, MLSTM_HEADS), F32))
            (C_p, n_p, m_p), h = run_prompt(mlstm_step, st0, rec, CHUNK)
            hp = hp + odd_output(h, xc, z, *ow, dt)
            mC_p.append(C_p); mn_p.append(n_p); mm_p.append(m_p); cv_p.append(buf_p)
            rec, xc, z, buf_s = odd_project(rms_norm(hs, od_norm_w[e]), state_mlstm_conv[e], *pw)
            st = (state_mlstm_C[e].astype(F32), state_mlstm_n[e].astype(F32), state_mlstm_m[e].astype(F32))
            (C_s, n_s, m_s), h = mlstm_step(st, *rec)
            hs = hs + odd_output(h, xc, z, *ow, dt)
            mC_s.append(C_s); mn_s.append(n_s); mm_s.append(m_s); cv_s.append(buf_s)

    y_prompt = rms_norm(hp, final_norm_w)[:, N_META:]
    y_sample = rms_norm(hs, final_norm_w)
    return (y_prompt, y_sample,
            jnp.stack(ckv_p), jnp.stack(kr_p), jnp.stack(hg_p),
            jnp.stack(mC_p), jnp.stack(mn_p), jnp.stack(mm_p), jnp.stack(cv_p),
            jnp.stack(ckv_s), jnp.stack(kr_s), jnp.stack(hg_s),
            jnp.stack(mC_s), jnp.stack(mn_s), jnp.stack(mm_s), jnp.stack(cv_s))
```

```python
import functools

import jax
import jax.numpy as jnp
from jax import lax
from jax.experimental import pallas as pl
from jax.experimental.pallas import tpu as pltpu

F32 = jnp.float32
BF16 = jnp.bfloat16

N_META = 16
HGRN_HEADS = 8
HGRN_DK = 128
MLA_HEADS = 8
MLA_NOPE = 128
MLA_ROPE = 64
MLA_V = 128
Q_LORA = 512
KV_LORA = 256
ROPE_THETA = 10000.0
MLSTM_HEADS = 4
QKV_BLOCK = 4
CONV_W = 4
NEG_BIG = -1e30
RMS_EPS = 1e-6
LN_EPS = 1e-5

LANES = 128
ROW_ALIGN = 256
ROW_BLOCK = 256
QK_WIDTH = KV_LORA + LANES
Q_BLOCK = 128
KV_BLOCK = 256
CHUNK = 128
SUB = 16
PAGES_PER_STEP = 8
BD_CHUNK = 256
VMEM_LIMIT = 56 * 1024 * 1024

_NT = (((1,), (1,)), ((), ()))
_TN = (((0,), (0,)), ((), ()))


def _dot(a, b):
    return jnp.dot(a, b, preferred_element_type=F32)


def _dot_nt(a, b):
    return lax.dot_general(a, b, _NT, preferred_element_type=F32)


def _dot_tn(a, b):
    return lax.dot_general(a, b, _TN, preferred_element_type=F32)


def _params(sem):
    return pltpu.CompilerParams(dimension_semantics=sem, vmem_limit_bytes=VMEM_LIMIT)


def _rms(x, w, eps=RMS_EPS):
    return x * lax.rsqrt(jnp.mean(x * x, axis=-1, keepdims=True) + eps) * w


def _silu(x):
    return x * jax.nn.sigmoid(x)


def _const_spec(shape):
    nd = len(shape)
    return pl.BlockSpec(shape, lambda *_: (0,) * nd, pipeline_mode=pl.Buffered(1))


def _norm_matmul_kernel(x_ref, nw_ref, w_ref, o_ref, *, chunks):
    xn = _rms(x_ref[...], nw_ref[...]).astype(BF16)
    for start, width in chunks:
        o_ref[:, start:start + width] = _dot(xn, w_ref[:, start:start + width])


def norm_matmul(x, nw, w):
    R, D = x.shape
    N = w.shape[1]
    tm = ROW_BLOCK
    chunks, s = [], 0
    while s < N:
        width = min(1024, N - s)
        chunks.append((s, width))
        s += width
    return pl.pallas_call(
        functools.partial(_norm_matmul_kernel, chunks=tuple(chunks)),
        out_shape=jax.ShapeDtypeStruct((R, N), F32),
        grid=(R // tm,),
        in_specs=[pl.BlockSpec((tm, D), lambda i: (i, 0)), _const_spec((1, D)), _const_spec((D, N))],
        out_specs=pl.BlockSpec((tm, N), lambda i: (i, 0)),
        compiler_params=_params(("parallel",)),
        name="norm_matmul",
    )(x, nw, w)


def _rope_tile(x, cos, sin):
    lane = lax.broadcasted_iota(jnp.int32, x.shape, 1)
    half = MLA_ROPE // 2
    partner = jnp.where(lane % MLA_ROPE < half, pltpu.roll(x, LANES - half, 1), pltpu.roll(x, half, 1))
    return x * cos + partner * sin


def _mla_prep_kernel(cq_ref, ckv_ref, kpe_ref, cos_ref, sin_ref, qnw_ref, wqb_ref, wuk_ref, kvnw_ref,
                     q_ref, k_ref, c_ref, kr_ref):
    cos = cos_ref[...]
    sin = sin_ref[...]
    cqn = _rms(cq_ref[...], qnw_ref[...]).astype(BF16)
    qb = _dot(cqn, wqb_ref[...])
    nope_w = MLA_HEADS * MLA_NOPE
    for h in range(MLA_HEADS):
        qn = qb[:, h * MLA_NOPE:(h + 1) * MLA_NOPE].astype(BF16)
        q_ref[h, :, 0:KV_LORA] = _dot(qn, wuk_ref[h]).astype(BF16)
        qpe = qb[:, nope_w + h * LANES: nope_w + (h + 1) * LANES]
        q_ref[h, :, KV_LORA:QK_WIDTH] = _rope_tile(qpe, cos, sin).astype(BF16)
    c = _rms(ckv_ref[...], kvnw_ref[...])
    kr = _rope_tile(kpe_ref[...], cos, sin)
    c_ref[...] = c
    kr_ref[...] = kr[:, 0:MLA_ROPE]
    k_ref[:, 0:KV_LORA] = c.astype(BF16)
    k_ref[:, KV_LORA:QK_WIDTH] = kr.astype(BF16)


def mla_prep(u, cos, sin, qnw, wqb, wuk_t, kvnw, col0):
    R = u.shape[0]
    tm = ROW_BLOCK
    cq_blk = col0 // Q_LORA
    ckv_blk = (col0 + Q_LORA) // KV_LORA
    kpe_blk = (col0 + Q_LORA + KV_LORA) // LANES
    return pl.pallas_call(
        _mla_prep_kernel,
        out_shape=(jax.ShapeDtypeStruct((MLA_HEADS, R, QK_WIDTH), BF16),
                   jax.ShapeDtypeStruct((R, QK_WIDTH), BF16),
                   jax.ShapeDtypeStruct((R, KV_LORA), F32),
                   jax.ShapeDtypeStruct((R, MLA_ROPE), F32)),
        grid=(R // tm,),
        in_specs=[pl.BlockSpec((tm, Q_LORA), lambda i: (i, cq_blk)),
                  pl.BlockSpec((tm, KV_LORA), lambda i: (i, ckv_blk)),
                  pl.BlockSpec((tm, LANES), lambda i: (i, kpe_blk)),
                  pl.BlockSpec((tm, LANES), lambda i: (i, 0)),
                  pl.BlockSpec((tm, LANES), lambda i: (i, 0)),
                  _const_spec(qnw.shape), _const_spec(wqb.shape), _const_spec(wuk_t.shape), _const_spec(kvnw.shape)],
        out_specs=(pl.BlockSpec((MLA_HEADS, tm, QK_WIDTH), lambda i: (0, i, 0)),
                   pl.BlockSpec((tm, QK_WIDTH), lambda i: (i, 0)),
                   pl.BlockSpec((tm, KV_LORA), lambda i: (i, 0)),
                   pl.BlockSpec((tm, MLA_ROPE), lambda i: (i, 0))),
        compiler_params=_params(("parallel",)),
        name="mla_prep",
    )(u, u, u, cos, sin, qnw, wqb, wuk_t, kvnw)


def _even_out_kernel(x_ref, oa_ref, ga_ref, gb_ref, ob_ref, hnw_ref, wuv_ref, wout_ref, o_ref, y_scr):
    hw = HGRN_HEADS * HGRN_DK
    for h in range(HGRN_HEADS):
        sl = slice(h * HGRN_DK, (h + 1) * HGRN_DK)
        ya = _rms(oa_ref[:, sl], hnw_ref[:, sl])
        y_scr[:, sl] = (ya * _silu(ga_ref[:, sl])).astype(BF16)
    for h in range(MLA_HEADS):
        sl = slice(h * MLA_V, (h + 1) * MLA_V)
        yb = _dot(ob_ref[h].astype(BF16), wuv_ref[h])
        y_scr[:, hw + h * MLA_V: hw + (h + 1) * MLA_V] = (yb * _silu(gb_ref[:, sl])).astype(BF16)
    o_ref[...] = x_ref[...] + _dot(y_scr[...], wout_ref[...])


def even_out(x, o_a, u, o_b, hnw, wuv, wout, ga_blk, gb_blk):
    R, D = x.shape
    tm = ROW_BLOCK
    W = HGRN_HEADS * HGRN_DK
    return pl.pallas_call(
        _even_out_kernel,
        out_shape=jax.ShapeDtypeStruct((R, D), F32),
        grid=(R // tm,),
        in_specs=[pl.BlockSpec((tm, D), lambda i: (i, 0)),
                  pl.BlockSpec((tm, W), lambda i: (i, 0)),
                  pl.BlockSpec((tm, W), lambda i: (i, ga_blk)),
                  pl.BlockSpec((tm, W), lambda i: (i, gb_blk)),
                  pl.BlockSpec((MLA_HEADS, tm, KV_LORA), lambda i: (0, i, 0)),
                  _const_spec(hnw.shape), _const_spec(wuv.shape), _const_spec(wout.shape)],
        out_specs=pl.BlockSpec((tm, D), lambda i: (i, 0)),
        scratch_shapes=[pltpu.VMEM((tm, 2 * W), BF16)],
        compiler_params=_params(("parallel",)),
        name="even_out",
    )(x, o_a, u, u, o_b, hnw, wuv, wout)


def _odd_mid_kernel(*refs, tm, blocks_per_seq, shifted_inputs, k_scale):
    if shifted_inputs:
        x_ref, x1_ref, x2_ref, x3_ref = refs[:4]
        rest = refs[4:]
    else:
        x_ref, halo_ref = refs[:2]
        rest = refs[2:]
    cw_ref, cb_ref, wq_ref, wk_ref, wv_ref, wg_ref, bg_ref, xc_ref, q_ref, k_ref, v_ref, g_ref = rest[:12]
    x = x_ref[...]
    if shifted_inputs:
        shifted = [x1_ref[...], x2_ref[...], x3_ref[...]]
    else:
        xs = rest[12]
        first = (pl.program_id(0) % blocks_per_seq) == 0
        halo = halo_ref[...]
        xs[0:8, :] = jnp.where(first, jnp.zeros_like(halo), halo)
        xs[8:, :] = x
        shifted = [xs[pl.ds(8 - s, tm), :] for s in (1, 2, 3)]
    cw = cw_ref[...]
    pre = cb_ref[...] + x * cw[3:4] + shifted[0] * cw[2:3] + shifted[1] * cw[1:2] + shifted[2] * cw[0:1]
    xc = _silu(pre)
    xc_ref[...] = xc
    D = x.shape[1]
    g = jnp.zeros((x.shape[0], LANES), F32) + bg_ref[...]
    for c in range(D // BD_CHUNK):
        sl = slice(c * BD_CHUNK, (c + 1) * BD_CHUNK)
        xcb = xc[:, sl].astype(BF16)
        q = _dot(xcb, wq_ref[c])
        k = _dot(xcb, wk_ref[c])
        v = _dot(x[:, sl].astype(BF16), wv_ref[c])
        q_ref[:, sl] = q
        k_ref[:, sl] = k * k_scale
        v_ref[:, sl] = v
        g = g + _dot(q.astype(BF16), wg_ref[c * BD_CHUNK:(c + 1) * BD_CHUNK, :])
        g = g + _dot(k.astype(BF16), wg_ref[D + c * BD_CHUNK: D + (c + 1) * BD_CHUNK, :])
        g = g + _dot(v.astype(BF16), wg_ref[2 * D + c * BD_CHUNK: 2 * D + (c + 1) * BD_CHUNK, :])
    g_ref[...] = g


def odd_mid(u, shifts, rows_per_seq, cw, cb, wq, wk, wv, wg, bg, k_scale):
    R = u.shape[0]
    D = u.shape[1] // 2
    tm = ROW_BLOCK
    row = lambda i: (i, 0)
    wspecs = [_const_spec(a.shape) for a in (cw, cb, wq, wk, wv, wg, bg)]
    outs = tuple(jax.ShapeDtypeStruct((R, D), F32) for _ in range(4)) + (jax.ShapeDtypeStruct((R, LANES), F32),)
    out_specs = tuple(pl.BlockSpec((tm, D), row) for _ in range(4)) + (pl.BlockSpec((tm, LANES), row),)
    if shifts is None:
        ins = [pl.BlockSpec((tm, D), row),
               pl.BlockSpec((8, D), lambda i: (jnp.maximum(i * (tm // 8) - 1, 0), 0))]
        args = (u, u)
        scratch = [pltpu.VMEM((tm + 8, D), F32)]
    else:
        ins = [pl.BlockSpec((tm, D), row)] * 4
        args = (u,) + tuple(shifts)
        scratch = []
    kern = functools.partial(_odd_mid_kernel, tm=tm, blocks_per_seq=rows_per_seq // tm,
                             shifted_inputs=shifts is not None, k_scale=k_scale)
    return pl.pallas_call(
        kern, out_shape=outs, grid=(R // tm,), in_specs=ins + wspecs, out_specs=out_specs,
        scratch_shapes=scratch, compiler_params=_params(("parallel",)), name="odd_mid",
    )(*args, cw, cb, wq, wk, wv, wg, bg)


def _odd_out_kernel(x_ref, h_ref, xc_ref, z_ref, nw_ref, skip_ref, wout_ref, o_ref, y_scr, *, dh):
    D = h_ref.shape[1]
    for hd in range(D // dh):
        sl = slice(hd * dh, (hd + 1) * dh)
        h = h_ref[:, sl]
        mu = jnp.mean(h, axis=-1, keepdims=True)
        hc = h - mu
        var = jnp.mean(hc * hc, axis=-1, keepdims=True)
        hn = hc * lax.rsqrt(var + LN_EPS) * nw_ref[:, sl]
        y = (hn + skip_ref[:, sl] * xc_ref[:, sl]) * _silu(z_ref[:, sl])
        y_scr[:, sl] = y.astype(BF16)
    o_ref[...] = x_ref[...] + _dot(y_scr[...], wout_ref[...])


def odd_out(x, h, xc, u, nw, skip, wout):
    R, Dm = x.shape
    D = h.shape[1]
    tm = ROW_BLOCK
    row = lambda i: (i, 0)
    return pl.pallas_call(
        functools.partial(_odd_out_kernel, dh=D // MLSTM_HEADS),
        out_shape=jax.ShapeDtypeStruct((R, Dm), F32),
        grid=(R // tm,),
        in_specs=[pl.BlockSpec((tm, Dm), row), pl.BlockSpec((tm, D), row), pl.BlockSpec((tm, D), row),
                  pl.BlockSpec((tm, D), lambda i: (i, 1)),
                  _const_spec(nw.shape), _const_spec(skip.shape), _const_spec(wout.shape)],
        out_specs=pl.BlockSpec((tm, Dm), row),
        scratch_shapes=[pltpu.VMEM((tm, D), BF16)],
        compiler_params=_params(("parallel",)),
        name="odd_out",
    )(x, h, xc, u, nw, skip, wout)


def _final_norm_kernel(x_ref, w_ref, o_ref):
    o_ref[...] = _rms(x_ref[...], w_ref[...])


def final_norm(x, w, skip_blocks, blocks_per_seq_in, blocks_per_seq_out):
    R, D = x.shape
    tm = ROW_BLOCK
    n_seq = R // (blocks_per_seq_in * tm)
    Ro = n_seq * blocks_per_seq_out * tm

    def in_map(i):
        return ((i // blocks_per_seq_out) * blocks_per_seq_in + skip_blocks + i % blocks_per_seq_out, 0)

    return pl.pallas_call(
        _final_norm_kernel,
        out_shape=jax.ShapeDtypeStruct((Ro, D), F32),
        grid=(Ro // tm,),
        in_specs=[pl.BlockSpec((tm, D), in_map), _const_spec(w.shape)],
        out_specs=pl.BlockSpec((tm, D), lambda i: (i, 0)),
        compiler_params=_params(("parallel",)),
        name="final_norm",
    )(x, w)


def _split3(x):
    hi = x.astype(BF16)
    r1 = x - hi.astype(F32)
    mid = r1.astype(BF16)
    lo = (r1 - mid.astype(F32)).astype(BF16)
    return hi, mid, lo


def _cumsum_rows(x):
    L = x.shape[0]
    r = lax.broadcasted_iota(jnp.int32, (L, L), 0)
    c = lax.broadcasted_iota(jnp.int32, (L, L), 1)
    tri = (c <= r).astype(BF16)
    hi, mid, lo = _split3(x)
    return _dot(tri, hi) + _dot(tri, mid) + _dot(tri, lo)


def _row_to_col(row):
    n = row.shape[1]
    r = lax.broadcasted_iota(jnp.int32, (n, n), 0)
    c = lax.broadcasted_iota(jnp.int32, (n, n), 1)
    return jnp.sum(jnp.where(r == c, row, 0.0), axis=1, keepdims=True)


def _hgrn_chunk_kernel(qa_ref, fa_ref, ia_ref, lb_ref, s0_ref, o_ref, sout_ref, s_scr, *, Lc, sub):
    @pl.when(pl.program_id(2) == 0)
    def _():
        s_scr[...] = s0_ref[0, 0]

    qa = qa_ref[0]
    fa = fa_ref[0]
    v = ia_ref[0]
    lb = lb_ref[...]
    q = _silu(qa)
    logf = jnp.log(lb + (1.0 - lb) * jax.nn.sigmoid(fa))
    k = (1.0 - lb) * jax.nn.sigmoid(-fa)
    G = _cumsum_rows(logf)
    vb = v.astype(BF16)
    S = s_scr[...]
    Sb = S.astype(BF16)
    tr = lax.broadcasted_iota(jnp.int32, (sub, sub), 0)
    tc = lax.broadcasted_iota(jnp.int32, (sub, sub), 1)
    for i in range(Lc // sub):
        lo = i * sub
        Gi = G[lo:lo + sub]
        qi = q[lo:lo + sub]
        ki = k[lo:lo + sub]
        e = jnp.exp(jnp.minimum(Gi[:, None, :] - Gi[None, :, :], 0.0))
        ad = jnp.sum(qi[:, None, :] * ki[None, :, :] * e, axis=-1)
        ad = jnp.where(tc <= tr, ad, 0.0)
        o = _dot(ad.astype(BF16), vb[lo:lo + sub]) + _dot((qi * jnp.exp(Gi)).astype(BF16), Sb)
        if i > 0:
            r = G[lo - 1:lo]
            qd = (qi * jnp.exp(Gi - r)).astype(BF16)
            kd = (k[:lo] * jnp.exp(r - G[:lo])).astype(BF16)
            o = o + _dot(_dot_nt(qd, kd).astype(BF16), vb[:lo])
        o_ref[0, lo:lo + sub, :] = o
    GL = G[Lc - 1:Lc]
    kd_all = (k * jnp.exp(GL - G)).astype(BF16)
    S_new = _row_to_col(jnp.exp(GL)) * S + _dot_tn(kd_all, vb)
    s_scr[...] = S_new
    sout_ref[0, 0] = S_new


def hgrn_scan(u3, lb, s0, Lc):
    B, T, _ = u3.shape
    H, K = HGRN_HEADS, HGRN_DK
    nc = T // Lc
    sub = min(SUB, Lc)
    kern = functools.partial(_hgrn_chunk_kernel, Lc=Lc, sub=sub)
    return pl.pallas_call(
        kern,
        out_shape=(jax.ShapeDtypeStruct((B, T, H * K), F32), jax.ShapeDtypeStruct((B, H, K, K), F32)),
        grid=(B, H, nc),
        in_specs=[pl.BlockSpec((1, Lc, K), lambda b, h, c: (b, c, h)),
                  pl.BlockSpec((1, Lc, K), lambda b, h, c: (b, c, H + h)),
                  pl.BlockSpec((1, Lc, K), lambda b, h, c: (b, c, 2 * H + h)),
                  pl.BlockSpec((1, K), lambda b, h, c: (0, h)),
                  pl.BlockSpec((1, 1, K, K), lambda b, h, c: (b, h, 0, 0))],
        out_specs=(pl.BlockSpec((1, Lc, K), lambda b, h, c: (b, c, h)),
                   pl.BlockSpec((1, 1, K, K), lambda b, h, c: (b, h, 0, 0))),
        scratch_shapes=[pltpu.VMEM((K, K), F32)],
        compiler_params=_params(("parallel", "parallel", "arbitrary")),
        name="hgrn_scan",
    )(u3, u3, u3, lb, s0)


def _mlstm_chunk_kernel(q_ref, k_ref, v_ref, g_ref, c0_ref, n0_ref, m0_ref, h_ref, cout_ref, nout_ref, mout_ref,
                        c_scr, n_scr, m_scr, *, Lc, pad):
    hd = pl.program_id(1)
    ci = pl.program_id(2)

    @pl.when(ci == 0)
    def _():
        c_scr[...] = c0_ref[0, 0]
        n_scr[...] = n0_ref[0, 0]
        m_scr[...] = m0_ref[0, 0]

    q = q_ref[0]
    ks = k_ref[0]
    v = v_ref[0]
    g = g_ref[0]
    lane = lax.broadcasted_iota(jnp.int32, g.shape, 1)
    logi = jnp.sum(jnp.where(lane == hd, g, 0.0), axis=1, keepdims=True)
    fpre = jnp.sum(jnp.where(lane == hd + MLSTM_HEADS, g, 0.0), axis=1, keepdims=True)
    logf = jnp.minimum(fpre, 0.0) - jnp.log1p(jnp.exp(-jnp.abs(fpre)))
    t = ci * Lc + lax.broadcasted_iota(jnp.int32, (Lc, 1), 0)
    valid = t >= pad
    logf = jnp.where(valid, logf, 0.0)
    logi = jnp.where(valid, logi, NEG_BIG)
    r = lax.broadcasted_iota(jnp.int32, (Lc, Lc), 0)
    c = lax.broadcasted_iota(jnp.int32, (Lc, Lc), 1)
    eye = r == c
    tril = c <= r
    logf_row = jnp.sum(jnp.where(eye, logf, 0.0), axis=0, keepdims=True)
    logi_row = jnp.sum(jnp.where(eye, logi, 0.0), axis=0, keepdims=True)
    b_col = jnp.sum(jnp.where(tril, logf_row, 0.0), axis=1, keepdims=True)
    b_row = jnp.sum(jnp.where(c >= r, logf, 0.0), axis=0, keepdims=True)
    m0 = m_scr[:, 0:1]
    D = jnp.where(tril, b_col - b_row + logi_row, NEG_BIG)
    inter = b_col + m0
    m_t = jnp.maximum(inter, jnp.max(D, axis=1, keepdims=True))
    w_inter = jnp.exp(inter - m_t)
    qb = q.astype(BF16)
    kb = ks.astype(BF16)
    vb = v.astype(BF16)
    s = _dot_nt(qb, kb) * jnp.exp(D - m_t)
    C = c_scr[...]
    n = n_scr[...]
    num = _dot(s.astype(BF16), vb) + w_inter * _dot(qb, C.astype(BF16))
    den = jnp.sum(s, axis=1, keepdims=True) + w_inter * jnp.sum(q * n, axis=1, keepdims=True)
    h_ref[0] = num / jnp.maximum(jnp.abs(den), jnp.exp(-m_t))
    m_new = m_t[Lc - 1:Lc]
    bL = b_col[Lc - 1:Lc]
    carry = jnp.exp(bL + m0 - m_new)
    wj = jnp.exp(bL - b_col + logi - m_new)
    kw = ks * wj
    C_new = carry * C + _dot_tn(kw.astype(BF16), vb)
    n_new = carry * n + jnp.sum(kw, axis=0, keepdims=True)
    m_b = jnp.zeros((1, LANES), F32) + m_new
    c_scr[...] = C_new
    n_scr[...] = n_new
    m_scr[...] = m_b
    cout_ref[0, 0] = C_new
    nout_ref[0, 0] = n_new
    mout_ref[0, 0] = m_b


def mlstm_scan(q3, k3, v3, g3, c0, n0, m0, Lc, pad):
    B, T, D = q3.shape
    H = MLSTM_HEADS
    dh = D // H
    nc = T // Lc
    seq = lambda b, h, c: (b, c, h)
    st = lambda b, h, c: (b, h, 0, 0)
    kern = functools.partial(_mlstm_chunk_kernel, Lc=Lc, pad=pad)
    return pl.pallas_call(
        kern,
        out_shape=(jax.ShapeDtypeStruct((B, T, D), F32), jax.ShapeDtypeStruct((B, H, dh, dh), F32),
                   jax.ShapeDtypeStruct((B, H, 1, dh), F32), jax.ShapeDtypeStruct((B, H, 1, LANES), F32)),
        grid=(B, H, nc),
        in_specs=[pl.BlockSpec((1, Lc, dh), seq), pl.BlockSpec((1, Lc, dh), seq), pl.BlockSpec((1, Lc, dh), seq),
                  pl.BlockSpec((1, Lc, LANES), lambda b, h, c: (b, c, 0)),
                  pl.BlockSpec((1, 1, dh, dh), st), pl.BlockSpec((1, 1, 1, dh), st), pl.BlockSpec((1, 1, 1, LANES), st)],
        out_specs=(pl.BlockSpec((1, Lc, dh), seq), pl.BlockSpec((1, 1, dh, dh), st),
                   pl.BlockSpec((1, 1, 1, dh), st), pl.BlockSpec((1, 1, 1, LANES), st)),
        scratch_shapes=[pltpu.VMEM((dh, dh), F32), pltpu.VMEM((1, dh), F32), pltpu.VMEM((1, LANES), F32)],
        compiler_params=_params(("parallel", "parallel", "arbitrary")),
        name="mlstm_scan",
    )(q3, k3, v3, g3, c0, n0, m0)


def _attn_prompt_kernel(q_ref, k_ref, o_ref, m_scr, l_scr, acc_scr, *, tq, tk, pad, scale):
    i = pl.program_id(1)
    rows = MLA_HEADS * tq
    q = q_ref[...].reshape(rows, QK_WIDTH)
    qpos = i * tq + lax.broadcasted_iota(jnp.int32, (rows, 1), 0) % tq
    m_scr[...] = jnp.full((rows, 1), NEG_BIG, F32)
    l_scr[...] = jnp.zeros((rows, 1), F32)
    acc_scr[...] = jnp.zeros((rows, KV_LORA), F32)
    n_kv = ((i + 1) * tq + tk - 1) // tk

    def body(j, carry):
        kb = k_ref[pl.ds(pl.multiple_of(j * tk, tk), tk), :]
        s = _dot_nt(q, kb)
        kpos = j * tk + lax.broadcasted_iota(jnp.int32, (1, tk), 1)
        s = jnp.where((kpos <= qpos) & (kpos >= pad), s * scale, NEG_BIG)
        m_old = m_scr[...]
        m_new = jnp.maximum(m_old, jnp.max(s, axis=1, keepdims=True))
        alpha = jnp.exp(m_old - m_new)
        p = jnp.exp(s - m_new)
        l_scr[...] = alpha * l_scr[...] + jnp.sum(p, axis=1, keepdims=True)
        acc_scr[...] = alpha * acc_scr[...] + _dot(p.astype(BF16), kb[:, 0:KV_LORA])
        m_scr[...] = m_new
        return carry

    lax.fori_loop(0, n_kv, body, 0)
    o_ref[...] = (acc_scr[...] / l_scr[...]).reshape(MLA_HEADS, tq, KV_LORA)


def attn_prompt(Q, K, B, Tp, pad, scale):
    tq, tk = Q_BLOCK, KV_BLOCK
    nq = Tp // tq
    rows = MLA_HEADS * tq
    kern = functools.partial(_attn_prompt_kernel, tq=tq, tk=tk, pad=pad, scale=scale)
    return pl.pallas_call(
        kern,
        out_shape=jax.ShapeDtypeStruct((MLA_HEADS, B * Tp, KV_LORA), F32),
        grid=(B, nq),
        in_specs=[pl.BlockSpec((MLA_HEADS, tq, QK_WIDTH), lambda b, i: (0, b * nq + i, 0)),
                  pl.BlockSpec((Tp, QK_WIDTH), lambda b, i: (b, 0))],
        out_specs=pl.BlockSpec((MLA_HEADS, tq, KV_LORA), lambda b, i: (0, b * nq + i, 0)),
        scratch_shapes=[pltpu.VMEM((rows, 1), F32), pltpu.VMEM((rows, 1), F32), pltpu.VMEM((rows, KV_LORA), F32)],
        compiler_params=_params(("parallel", "arbitrary")),
        name="attn_prompt",
    )(Q, K)


def _attn_decode_kernel(pt_ref, q_ref, kn_ref, *rest, pages, L, scale):
    ckv_refs = rest[:pages]
    kr_refs = rest[pages:2 * pages]
    o_ref, m_scr, l_scr, acc_scr = rest[2 * pages:]
    g = pl.program_id(1)
    rows = q_ref.shape[1]

    @pl.when(g == 0)
    def _():
        m_scr[...] = jnp.full((rows, 1), NEG_BIG, F32)
        l_scr[...] = jnp.zeros((rows, 1), F32)
        acc_scr[...] = jnp.zeros((rows, KV_LORA), F32)

    q = q_ref[0]
    ql = q[:, 0:KV_LORA]
    qr = q[:, KV_LORA:KV_LORA + MLA_ROPE]

    def update(s, vals):
        m_old = m_scr[...]
        m_new = jnp.maximum(m_old, jnp.max(s, axis=1, keepdims=True))
        alpha = jnp.exp(m_old - m_new)
        p = jnp.exp(s - m_new)
        l_scr[...] = alpha * l_scr[...] + jnp.sum(p, axis=1, keepdims=True)
        acc_scr[...] = alpha * acc_scr[...] + _dot(p.astype(BF16), vals)
        m_scr[...] = m_new

    for sidx in range(pages):
        kc = ckv_refs[sidx][...].astype(BF16)
        kr = kr_refs[sidx][...].astype(BF16)
        update((_dot_nt(ql, kc) + _dot_nt(qr, kr)) * scale, kc)

    @pl.when(g == pl.num_programs(1) - 1)
    def _():
        kn = kn_ref[0]
        s = _dot_nt(q, kn)
        tpos = lax.broadcasted_iota(jnp.int32, (rows, L), 0) % L
        jpos = lax.broadcasted_iota(jnp.int32, (rows, L), 1)
        update(jnp.where(jpos <= tpos, s * scale, NEG_BIG), kn[:, 0:KV_LORA])
        o_ref[0] = acc_scr[...] / l_scr[...]


def attn_decode(Qs, Kn, cache_ckv, cache_kr, page_table, layer, scale):
    Bd, rows, _ = Qs.shape
    L = Kn.shape[1]
    n_pages = page_table.shape[1]
    page = cache_ckv.shape[2]
    P = PAGES_PER_STEP
    kern = functools.partial(_attn_decode_kernel, pages=P, L=L, scale=scale)

    def page_spec(width, slot):
        return pl.BlockSpec((None, None, page, width), lambda b, g, pt: (layer, pt[b, g * P + slot], 0, 0))

    grid_spec = pltpu.PrefetchScalarGridSpec(
        num_scalar_prefetch=1,
        grid=(Bd, n_pages // P),
        in_specs=[pl.BlockSpec((1, rows, QK_WIDTH), lambda b, g, pt: (b, 0, 0)),
                  pl.BlockSpec((1, L, QK_WIDTH), lambda b, g, pt: (b, 0, 0))]
                 + [page_spec(KV_LORA, s) for s in range(P)] + [page_spec(MLA_ROPE, s) for s in range(P)],
        out_specs=pl.BlockSpec((1, rows, KV_LORA), lambda b, g, pt: (b, 0, 0)),
        scratch_shapes=[pltpu.VMEM((rows, 1), F32), pltpu.VMEM((rows, 1), F32), pltpu.VMEM((rows, KV_LORA), F32)],
    )
    return pl.pallas_call(
        kern,
        out_shape=jax.ShapeDtypeStruct((Bd, rows, KV_LORA), F32),
        grid_spec=grid_spec,
        compiler_params=_params(("parallel", "arbitrary")),
        name="attn_decode",
    )(page_table, Qs, Kn, *([cache_ckv] * P), *([cache_kr] * P))


def _rope_tables(pos):
    half = MLA_ROPE // 2
    inv = ROPE_THETA ** (-jnp.arange(half, dtype=F32) / half)
    ang = pos.astype(F32)[:, None] * inv[None, :]
    cos, sin = jnp.cos(ang), jnp.sin(ang)
    z = jnp.zeros((pos.shape[0], LANES - MLA_ROPE), F32)
    return jnp.concatenate([cos, cos, z], axis=1), jnp.concatenate([-sin, sin, z], axis=1)


def _expand_blockdiag(w):
    G = w.shape[0]
    per = BD_CHUNK // QKV_BLOCK
    w4 = w.reshape(G // per, per, QKV_BLOCK, QKV_BLOCK)
    eye = jnp.eye(per, dtype=w.dtype)
    dense = jnp.einsum('cgio,gh->cgiho', w4, eye)
    return dense.reshape(G // per, BD_CHUNK, BD_CHUNK).astype(BF16)


def _prep_even(e, ev_norm_w, ev_w_in, hgrn_norm_w, mla_q_norm_w, mla_w_q_b, mla_kv_norm_w, mla_w_uk, mla_w_uv, ev_w_out):
    W = HGRN_HEADS * HGRN_DK
    w = ev_w_in[e]
    o = 0
    parts = {}
    for name, width in (("q_a", W), ("f_a", W), ("i_a", W), ("g_a", W), ("cq", Q_LORA), ("ckv", KV_LORA),
                        ("kpe", MLA_ROPE), ("g_b", MLA_HEADS * MLA_V)):
        parts[name] = w[:, o:o + width]
        o += width
    zpad = jnp.zeros((w.shape[0], LANES - MLA_ROPE), w.dtype)
    w_in = jnp.concatenate([parts[n] for n in ("q_a", "f_a", "i_a", "g_a", "g_b", "cq", "ckv", "kpe")] + [zpad], axis=1)
    wqb = mla_w_q_b[e].reshape(Q_LORA, MLA_HEADS, MLA_NOPE + MLA_ROPE)
    nope = wqb[:, :, :MLA_NOPE].reshape(Q_LORA, MLA_HEADS * MLA_NOPE)
    pe = jnp.pad(wqb[:, :, MLA_NOPE:], ((0, 0), (0, 0), (0, LANES - MLA_ROPE))).reshape(Q_LORA, MLA_HEADS * LANES)
    return dict(
        nw=ev_norm_w[e][None, :], w_in=w_in.astype(BF16),
        hnw=hgrn_norm_w[e][None, :], qnw=mla_q_norm_w[e][None, :], kvnw=mla_kv_norm_w[e][None, :],
        wqb=jnp.concatenate([nope, pe], axis=1).astype(BF16),
        wuk_t=jnp.transpose(mla_w_uk[e], (1, 2, 0)).astype(BF16),
        wuv=jnp.transpose(mla_w_uv[e], (1, 0, 2)).astype(BF16),
        wout=ev_w_out[e].astype(BF16),
    )


def _prep_odd(e, od_norm_w, od_w_in, conv_w, conv_b, w_q, w_k, w_v, w_gate, b_gate, norm_w, skip, od_w_out):
    ng = w_gate.shape[2]
    return dict(
        nw=od_norm_w[e][None, :], w_in=od_w_in[e].astype(BF16),
        cw=conv_w[e], cb=conv_b[e][None, :],
        wq=_expand_blockdiag(w_q[e]), wk=_expand_blockdiag(w_k[e]), wv=_expand_blockdiag(w_v[e]),
        wg=jnp.pad(w_gate[e], ((0, 0), (0, LANES - ng))).astype(BF16),
        bg=jnp.pad(b_gate[e], (0, LANES - ng))[None, :],
        lnw=norm_w[e][None, :], skip=skip[e][None, :], wout=od_w_out[e].astype(BF16),
    )


def _even_layer(x, p, lb, cos, sin, n_seq, seq_len, Lc, s0, attend):
    W = HGRN_HEADS * HGRN_DK
    u = norm_matmul(x, p["nw"], p["w_in"])
    Q, K, c, kr = mla_prep(u, cos, sin, p["qnw"], p["wqb"], p["wuk_t"], p["kvnw"], col0=5 * W)
    o_a, S = hgrn_scan(u.reshape(n_seq, seq_len, u.shape[1]), lb, s0, Lc)
    o_b = attend(Q, K)
    x = even_out(x, o_a.reshape(x.shape[0], W), u, o_b, p["hnw"], p["wuv"], p["wout"], ga_blk=3, gb_blk=4)
    return x, c, kr, S


def _odd_layer(x, p, n_seq, seq_len, Lc, pad, state, conv_buf):
    u = norm_matmul(x, p["nw"], p["w_in"])
    D = u.shape[1] // 2
    dh = D // MLSTM_HEADS
    if conv_buf is None:
        shifts = None
    else:
        xp = jnp.concatenate([conv_buf, u[:, :D].reshape(n_seq, seq_len, D)], axis=1)
        shifts = [xp[:, CONV_W - 1 - s: CONV_W - 1 - s + seq_len].reshape(n_seq * seq_len, D) for s in (1, 2, 3)]
    xc, q, ks, v, g = odd_mid(u, shifts, seq_len, p["cw"], p["cb"], p["wq"], p["wk"], p["wv"], p["wg"], p["bg"],
                              k_scale=dh ** -0.5)
    r3 = lambda a: a.reshape(n_seq, seq_len, a.shape[1])
    c0, n0, m0 = state
    h, C, n, m = mlstm_scan(r3(q), r3(ks), r3(v), r3(g), c0, n0[:, :, None, :],
                            jnp.broadcast_to(m0[:, :, None, None], m0.shape + (1, LANES)), Lc, pad)
    x = odd_out(x, h.reshape(x.shape[0], D), xc, u, p["lnw"], p["skip"], p["wout"])
    new_buf = u[:, :D].reshape(n_seq, seq_len, D)[:, seq_len - (CONV_W - 1):]
    return x, C, n[:, :, 0, :], m[:, :, 0, 0], new_buf


def kernel(x_prompt, x_sample, cache_mla_ckv, cache_mla_krope, state_hgrn, state_mlstm_C, state_mlstm_n, state_mlstm_m, state_mlstm_conv, page_table, meta_tokens, ev_norm_w, ev_w_in, hgrn_lower_bounds, hgrn_norm_w, mla_q_norm_w, mla_w_q_b, mla_kv_norm_w, mla_w_uk, mla_w_uv, ev_w_out, od_norm_w, od_w_in, mlstm_conv_w, mlstm_conv_b, mlstm_w_q, mlstm_w_k, mlstm_w_v, mlstm_w_gate, mlstm_b_gate, mlstm_norm_w, mlstm_skip, od_w_out, final_norm_w):
    B, seq, Dm = x_prompt.shape
    Bd, Ls, _ = x_sample.shape
    depth = ev_norm_w.shape[0] + od_norm_w.shape[0]
    T = seq + N_META
    Tp = -(-T // ROW_ALIGN) * ROW_ALIGN
    pad = Tp - T
    assert seq % ROW_BLOCK == 0 and (pad + N_META) % ROW_BLOCK == 0 and (Bd * Ls) % ROW_BLOCK == 0
    past_len = page_table.shape[1] * cache_mla_ckv.shape[2]
    scale = (MLA_NOPE + MLA_ROPE) ** -0.5
    dh = (od_w_in.shape[2] // 2) // MLSTM_HEADS

    hp = jnp.concatenate([jnp.zeros((B, pad, Dm), F32), jnp.broadcast_to(meta_tokens[None], (B, N_META, Dm)), x_prompt],
                         axis=1).reshape(B * Tp, Dm)
    hs = x_sample.reshape(Bd * Ls, Dm)
    cos_p, sin_p = _rope_tables(jnp.arange(Tp) - pad)
    cos_p, sin_p = jnp.tile(cos_p, (B, 1)), jnp.tile(sin_p, (B, 1))
    cos_s, sin_s = _rope_tables(past_len + jnp.arange(Ls))
    cos_s, sin_s = jnp.tile(cos_s, (Bd, 1)), jnp.tile(sin_s, (Bd, 1))
    sm = jax.nn.softmax(hgrn_lower_bounds.astype(F32), axis=0)
    lbs = jnp.cumsum(sm, axis=0) - sm[0]

    def attend_prompt(Q, K):
        return attn_prompt(Q, K, B, Tp, pad, scale)

    def make_attend_sample(e):
        def attend(Q, K):
            Qs = Q.reshape(MLA_HEADS, Bd, Ls, QK_WIDTH).transpose(1, 0, 2, 3).reshape(Bd, MLA_HEADS * Ls, QK_WIDTH)
            o = attn_decode(Qs, K.reshape(Bd, Ls, QK_WIDTH), cache_mla_ckv, cache_mla_krope, page_table, e, scale)
            return o.reshape(Bd, MLA_HEADS, Ls, KV_LORA).transpose(1, 0, 2, 3).reshape(MLA_HEADS, Bd * Ls, KV_LORA)
        return attend

    outs_p = {k: [] for k in ("ckv", "kr", "S", "C", "n", "m", "cv")}
    outs_s = {k: [] for k in ("ckv", "kr", "S", "C", "n", "m", "cv")}
    for l in range(depth):
        e = l // 2
        if l % 2 == 0:
            p = _prep_even(e, ev_norm_w, ev_w_in, hgrn_norm_w, mla_q_norm_w, mla_w_q_b, mla_kv_norm_w, mla_w_uk,
                           mla_w_uv, ev_w_out)
            lb = lbs[e][None, :]
            s0 = jnp.zeros((B, HGRN_HEADS, HGRN_DK, HGRN_DK), F32)
            hp, c, kr, S = _even_layer(hp, p, lb, cos_p, sin_p, B, Tp, CHUNK, s0, attend_prompt)
            outs_p["ckv"].append(c.reshape(B, Tp, KV_LORA)[:, pad:])
            outs_p["kr"].append(kr.reshape(B, Tp, MLA_ROPE)[:, pad:])
            outs_p["S"].append(S)
            hs, c, kr, S = _even_layer(hs, p, lb, cos_s, sin_s, Bd, Ls, Ls, state_hgrn[e], make_attend_sample(e))
            outs_s["ckv"].append(c.reshape(Bd, Ls, KV_LORA))
            outs_s["kr"].append(kr.reshape(Bd, Ls, MLA_ROPE))
            outs_s["S"].append(S)
        else:
            p = _prep_odd(e, od_norm_w, od_w_in, mlstm_conv_w, mlstm_conv_b, mlstm_w_q, mlstm_w_k, mlstm_w_v,
                          mlstm_w_gate, mlstm_b_gate, mlstm_norm_w, mlstm_skip, od_w_out)
            st0 = (jnp.zeros((B, MLSTM_HEADS, dh, dh), F32), jnp.zeros((B, MLSTM_HEADS, dh), F32),
                   jnp.zeros((B, MLSTM_HEADS), F32))
            hp, C, n, m, cv = _odd_layer(hp, p, B, Tp, CHUNK, pad, st0, None)
            for key, val in zip(("C", "n", "m", "cv"), (C, n, m, cv)):
                outs_p[key].append(val)
            st = (state_mlstm_C[e], state_mlstm_n[e], state_mlstm_m[e])
            hs, C, n, m, cv = _odd_layer(hs, p, Bd, Ls, Ls, 0, st, state_mlstm_conv[e])
            for key, val in zip(("C", "n", "m", "cv"), (C, n, m, cv)):
                outs_s[key].append(val)

    fw = final_norm_w[None, :]
    blocks_in = Tp // ROW_BLOCK
    skip_blocks = (pad + N_META) // ROW_BLOCK
    y_prompt = final_norm(hp, fw, skip_blocks, blocks_in, blocks_in - skip_blocks).reshape(B, seq, Dm)
    y_sample = final_norm(hs, fw, 0, (Bd * Ls) // ROW_BLOCK, (Bd * Ls) // ROW_BLOCK).reshape(Bd, Ls, Dm)
    st = lambda xs: jnp.stack(xs)
    return (y_prompt, y_sample,
            st(outs_p["ckv"]), st(outs_p["kr"]), st(outs_p["S"]), st(outs_p["C"]), st(outs_p["n"]), st(outs_p["m"]),
            st(outs_p["cv"]),
            st(outs_s["ckv"]), st(outs_s["kr"]), st(outs_s["S"]), st(outs_s["C"]), st(outs_s["n"]), st(outs_s["m"]),
            st(outs_s["cv"]))
```

```python
import functools

import jax
import jax.numpy as jnp
from jax import lax
from jax.experimental import pallas as pl
from jax.experimental.pallas import tpu as pltpu

F32 = jnp.float32
BF16 = jnp.bfloat16

N_META = 16
HGRN_HEADS = 8
HGRN_DK = 128
MLA_HEADS = 8
MLA_NOPE = 128
MLA_ROPE = 64
MLA_V = 128
Q_LORA = 512
KV_LORA = 256
ROPE_THETA = 10000.0
MLSTM_HEADS = 4
QKV_BLOCK = 4
CONV_W = 4
NEG_BIG = -1e30
RMS_EPS = 1e-6
LN_EPS = 1e-5

LANES = 128
ROW_ALIGN = 256
ROW_BLOCK = 256
QK_WIDTH = KV_LORA + LANES
Q_BLOCK = 128
KV_BLOCK = 512
HGRN_CHUNK = 128
MLSTM_CHUNK = 256
SUB = 16
PAGES_PER_STEP = 16
HGRN_SAFE_EXP = 60.0
BD_CHUNK = 256
VMEM_LIMIT = 56 * 1024 * 1024

_NT = (((1,), (1,)), ((), ()))
_TN = (((0,), (0,)), ((), ()))


def _dot(a, b):
    return jnp.dot(a, b, preferred_element_type=F32)


def _dot_nt(a, b):
    return lax.dot_general(a, b, _NT, preferred_element_type=F32)


def _dot_tn(a, b):
    return lax.dot_general(a, b, _TN, preferred_element_type=F32)


def _params(sem):
    return pltpu.CompilerParams(dimension_semantics=sem, vmem_limit_bytes=VMEM_LIMIT)


def _rms(x, w, eps=RMS_EPS):
    return x * lax.rsqrt(jnp.mean(x * x, axis=-1, keepdims=True) + eps) * w


def _silu(x):
    return x * jax.nn.sigmoid(x)


def _const_spec(shape):
    nd = len(shape)
    return pl.BlockSpec(shape, lambda *_: (0,) * nd, pipeline_mode=pl.Buffered(1))


def _norm_matmul_kernel(x_ref, nw_ref, w_ref, o_ref, *, chunks):
    xn = _rms(x_ref[...], nw_ref[...]).astype(BF16)
    for start, width in chunks:
        o_ref[:, start:start + width] = _dot(xn, w_ref[:, start:start + width])


def norm_matmul(x, nw, w):
    R, D = x.shape
    N = w.shape[1]
    tm = ROW_BLOCK
    chunks, s = [], 0
    while s < N:
        width = min(1024, N - s)
        chunks.append((s, width))
        s += width
    return pl.pallas_call(
        functools.partial(_norm_matmul_kernel, chunks=tuple(chunks)),
        out_shape=jax.ShapeDtypeStruct((R, N), F32),
        grid=(R // tm,),
        in_specs=[pl.BlockSpec((tm, D), lambda i: (i, 0)), _const_spec((1, D)), _const_spec((D, N))],
        out_specs=pl.BlockSpec((tm, N), lambda i: (i, 0)),
        compiler_params=_params(("parallel",)),
        name="norm_matmul",
    )(x, nw, w)


def _rope_tile(x, cos, sin):
    lane = lax.broadcasted_iota(jnp.int32, x.shape, 1)
    half = MLA_ROPE // 2
    partner = jnp.where(lane % MLA_ROPE < half, pltpu.roll(x, LANES - half, 1), pltpu.roll(x, half, 1))
    return x * cos + partner * sin


def _mla_prep_kernel(cq_ref, ckv_ref, kpe_ref, cos_ref, sin_ref, qnw_ref, wqb_ref, wuk_ref, kvnw_ref,
                     q_ref, k_ref, c_ref, kr_ref):
    cos = cos_ref[...]
    sin = sin_ref[...]
    cqn = _rms(cq_ref[...], qnw_ref[...]).astype(BF16)
    qb = _dot(cqn, wqb_ref[...])
    nope_w = MLA_HEADS * MLA_NOPE
    for h in range(MLA_HEADS):
        qn = qb[:, h * MLA_NOPE:(h + 1) * MLA_NOPE].astype(BF16)
        q_ref[h, :, 0:KV_LORA] = _dot(qn, wuk_ref[h]).astype(BF16)
        qpe = qb[:, nope_w + h * LANES: nope_w + (h + 1) * LANES]
        q_ref[h, :, KV_LORA:QK_WIDTH] = _rope_tile(qpe, cos, sin).astype(BF16)
    c = _rms(ckv_ref[...], kvnw_ref[...])
    kr = _rope_tile(kpe_ref[...], cos, sin)
    c_ref[...] = c
    kr_ref[...] = kr[:, 0:MLA_ROPE]
    k_ref[:, 0:KV_LORA] = c.astype(BF16)
    k_ref[:, KV_LORA:QK_WIDTH] = kr.astype(BF16)


def mla_prep(u, cos, sin, qnw, wqb, wuk_t, kvnw, col0):
    R = u.shape[0]
    tm = ROW_BLOCK
    cq_blk = col0 // Q_LORA
    ckv_blk = (col0 + Q_LORA) // KV_LORA
    kpe_blk = (col0 + Q_LORA + KV_LORA) // LANES
    return pl.pallas_call(
        _mla_prep_kernel,
        out_shape=(jax.ShapeDtypeStruct((MLA_HEADS, R, QK_WIDTH), BF16),
                   jax.ShapeDtypeStruct((R, QK_WIDTH), BF16),
                   jax.ShapeDtypeStruct((R, KV_LORA), F32),
                   jax.ShapeDtypeStruct((R, MLA_ROPE), F32)),
        grid=(R // tm,),
        in_specs=[pl.BlockSpec((tm, Q_LORA), lambda i: (i, cq_blk)),
                  pl.BlockSpec((tm, KV_LORA), lambda i: (i, ckv_blk)),
                  pl.BlockSpec((tm, LANES), lambda i: (i, kpe_blk)),
                  pl.BlockSpec((tm, LANES), lambda i: (i, 0)),
                  pl.BlockSpec((tm, LANES), lambda i: (i, 0)),
                  _const_spec(qnw.shape), _const_spec(wqb.shape), _const_spec(wuk_t.shape), _const_spec(kvnw.shape)],
        out_specs=(pl.BlockSpec((MLA_HEADS, tm, QK_WIDTH), lambda i: (0, i, 0)),
                   pl.BlockSpec((tm, QK_WIDTH), lambda i: (i, 0)),
                   pl.BlockSpec((tm, KV_LORA), lambda i: (i, 0)),
                   pl.BlockSpec((tm, MLA_ROPE), lambda i: (i, 0))),
        compiler_params=_params(("parallel",)),
        name="mla_prep",
    )(u, u, u, cos, sin, qnw, wqb, wuk_t, kvnw)


def _even_out_kernel(x_ref, oa_ref, ga_ref, gb_ref, ob_ref, hnw_ref, wuv_ref, wout_ref, o_ref, y_scr):
    hw = HGRN_HEADS * HGRN_DK
    for h in range(HGRN_HEADS):
        sl = slice(h * HGRN_DK, (h + 1) * HGRN_DK)
        ya = _rms(oa_ref[:, sl], hnw_ref[:, sl])
        y_scr[:, sl] = (ya * _silu(ga_ref[:, sl])).astype(BF16)
    for h in range(MLA_HEADS):
        sl = slice(h * MLA_V, (h + 1) * MLA_V)
        yb = _dot(ob_ref[h].astype(BF16), wuv_ref[h])
        y_scr[:, hw + h * MLA_V: hw + (h + 1) * MLA_V] = (yb * _silu(gb_ref[:, sl])).astype(BF16)
    o_ref[...] = x_ref[...] + _dot(y_scr[...], wout_ref[...])


def even_out(x, o_a, u, o_b, hnw, wuv, wout, ga_blk, gb_blk):
    R, D = x.shape
    tm = ROW_BLOCK
    W = HGRN_HEADS * HGRN_DK
    return pl.pallas_call(
        _even_out_kernel,
        out_shape=jax.ShapeDtypeStruct((R, D), F32),
        grid=(R // tm,),
        in_specs=[pl.BlockSpec((tm, D), lambda i: (i, 0)),
                  pl.BlockSpec((tm, W), lambda i: (i, 0)),
                  pl.BlockSpec((tm, W), lambda i: (i, ga_blk)),
                  pl.BlockSpec((tm, W), lambda i: (i, gb_blk)),
                  pl.BlockSpec((MLA_HEADS, tm, KV_LORA), lambda i: (0, i, 0)),
                  _const_spec(hnw.shape), _const_spec(wuv.shape), _const_spec(wout.shape)],
        out_specs=pl.BlockSpec((tm, D), lambda i: (i, 0)),
        scratch_shapes=[pltpu.VMEM((tm, 2 * W), BF16)],
        compiler_params=_params(("parallel",)),
        name="even_out",
    )(x, o_a, u, u, o_b, hnw, wuv, wout)


def _odd_mid_kernel(*refs, tm, blocks_per_seq, shifted_inputs, k_scale):
    if shifted_inputs:
        x_ref, x1_ref, x2_ref, x3_ref = refs[:4]
        rest = refs[4:]
    else:
        x_ref, halo_ref = refs[:2]
        rest = refs[2:]
    cw_ref, cb_ref, wq_ref, wk_ref, wv_ref, wg_ref, bg_ref, xc_ref, q_ref, k_ref, v_ref, g_ref = rest[:12]
    x = x_ref[...]
    if shifted_inputs:
        shifted = [x1_ref[...], x2_ref[...], x3_ref[...]]
    else:
        xs = rest[12]
        first = (pl.program_id(0) % blocks_per_seq) == 0
        halo = halo_ref[...]
        xs[0:8, :] = jnp.where(first, jnp.zeros_like(halo), halo)
        xs[8:, :] = x
        shifted = [xs[pl.ds(8 - s, tm), :] for s in (1, 2, 3)]
    cw = cw_ref[...]
    pre = cb_ref[...] + x * cw[3:4] + shifted[0] * cw[2:3] + shifted[1] * cw[1:2] + shifted[2] * cw[0:1]
    xc = _silu(pre)
    xc_ref[...] = xc
    D = x.shape[1]
    g = jnp.zeros((x.shape[0], LANES), F32) + bg_ref[...]
    for c in range(D // BD_CHUNK):
        sl = slice(c * BD_CHUNK, (c + 1) * BD_CHUNK)
        xcb = xc[:, sl].astype(BF16)
        q = _dot(xcb, wq_ref[c])
        k = _dot(xcb, wk_ref[c])
        v = _dot(x[:, sl].astype(BF16), wv_ref[c])
        q_ref[:, sl] = q
        k_ref[:, sl] = k * k_scale
        v_ref[:, sl] = v
        g = g + _dot(q.astype(BF16), wg_ref[c * BD_CHUNK:(c + 1) * BD_CHUNK, :])
        g = g + _dot(k.astype(BF16), wg_ref[D + c * BD_CHUNK: D + (c + 1) * BD_CHUNK, :])
        g = g + _dot(v.astype(BF16), wg_ref[2 * D + c * BD_CHUNK: 2 * D + (c + 1) * BD_CHUNK, :])
    g_ref[...] = g


def odd_mid(u, shifts, rows_per_seq, cw, cb, wq, wk, wv, wg, bg, k_scale):
    R = u.shape[0]
    D = u.shape[1] // 2
    tm = ROW_BLOCK
    row = lambda i: (i, 0)
    wspecs = [_const_spec(a.shape) for a in (cw, cb, wq, wk, wv, wg, bg)]
    outs = tuple(jax.ShapeDtypeStruct((R, D), F32) for _ in range(4)) + (jax.ShapeDtypeStruct((R, LANES), F32),)
    out_specs = tuple(pl.BlockSpec((tm, D), row) for _ in range(4)) + (pl.BlockSpec((tm, LANES), row),)
    if shifts is None:
        ins = [pl.BlockSpec((tm, D), row),
               pl.BlockSpec((8, D), lambda i: (jnp.maximum(i * (tm // 8) - 1, 0), 0))]
        args = (u, u)
        scratch = [pltpu.VMEM((tm + 8, D), F32)]
    else:
        ins = [pl.BlockSpec((tm, D), row)] * 4
        args = (u,) + tuple(shifts)
        scratch = []
    kern = functools.partial(_odd_mid_kernel, tm=tm, blocks_per_seq=rows_per_seq // tm,
                             shifted_inputs=shifts is not None, k_scale=k_scale)
    return pl.pallas_call(
        kern, out_shape=outs, grid=(R // tm,), in_specs=ins + wspecs, out_specs=out_specs,
        scratch_shapes=scratch, compiler_params=_params(("parallel",)), name="odd_mid",
    )(*args, cw, cb, wq, wk, wv, wg, bg)


def _odd_out_kernel(x_ref, h_ref, xc_ref, z_ref, nw_ref, skip_ref, wout_ref, o_ref, y_scr, *, dh):
    D = h_ref.shape[1]
    for hd in range(D // dh):
        sl = slice(hd * dh, (hd + 1) * dh)
        h = h_ref[:, sl]
        mu = jnp.mean(h, axis=-1, keepdims=True)
        hc = h - mu
        var = jnp.mean(hc * hc, axis=-1, keepdims=True)
        hn = hc * lax.rsqrt(var + LN_EPS) * nw_ref[:, sl]
        y = (hn + skip_ref[:, sl] * xc_ref[:, sl]) * _silu(z_ref[:, sl])
        y_scr[:, sl] = y.astype(BF16)
    o_ref[...] = x_ref[...] + _dot(y_scr[...], wout_ref[...])


def odd_out(x, h, xc, u, nw, skip, wout):
    R, Dm = x.shape
    D = h.shape[1]
    tm = ROW_BLOCK
    row = lambda i: (i, 0)
    return pl.pallas_call(
        functools.partial(_odd_out_kernel, dh=D // MLSTM_HEADS),
        out_shape=jax.ShapeDtypeStruct((R, Dm), F32),
        grid=(R // tm,),
        in_specs=[pl.BlockSpec((tm, Dm), row), pl.BlockSpec((tm, D), row), pl.BlockSpec((tm, D), row),
                  pl.BlockSpec((tm, D), lambda i: (i, 1)),
                  _const_spec(nw.shape), _const_spec(skip.shape), _const_spec(wout.shape)],
        out_specs=pl.BlockSpec((tm, Dm), row),
        scratch_shapes=[pltpu.VMEM((tm, D), BF16)],
        compiler_params=_params(("parallel",)),
        name="odd_out",
    )(x, h, xc, u, nw, skip, wout)


def _final_norm_kernel(x_ref, w_ref, o_ref):
    o_ref[...] = _rms(x_ref[...], w_ref[...])


def final_norm(x, w, skip_blocks, blocks_per_seq_in, blocks_per_seq_out):
    R, D = x.shape
    tm = ROW_BLOCK
    n_seq = R // (blocks_per_seq_in * tm)
    Ro = n_seq * blocks_per_seq_out * tm

    def in_map(i):
        return ((i // blocks_per_seq_out) * blocks_per_seq_in + skip_blocks + i % blocks_per_seq_out, 0)

    return pl.pallas_call(
        _final_norm_kernel,
        out_shape=jax.ShapeDtypeStruct((Ro, D), F32),
        grid=(Ro // tm,),
        in_specs=[pl.BlockSpec((tm, D), in_map), _const_spec(w.shape)],
        out_specs=pl.BlockSpec((tm, D), lambda i: (i, 0)),
        compiler_params=_params(("parallel",)),
        name="final_norm",
    )(x, w)


def _split3(x):
    hi = x.astype(BF16)
    r1 = x - hi.astype(F32)
    mid = r1.astype(BF16)
    lo = (r1 - mid.astype(F32)).astype(BF16)
    return hi, mid, lo


def _cumsum_rows(x):
    L = x.shape[0]
    r = lax.broadcasted_iota(jnp.int32, (L, L), 0)
    c = lax.broadcasted_iota(jnp.int32, (L, L), 1)
    tri = (c <= r).astype(BF16)
    hi, mid, lo = _split3(x)
    return _dot(tri, hi) + _dot(tri, mid) + _dot(tri, lo)


def _row_to_col(row):
    n = row.shape[1]
    r = lax.broadcasted_iota(jnp.int32, (n, n), 0)
    c = lax.broadcasted_iota(jnp.int32, (n, n), 1)
    return jnp.sum(jnp.where(r == c, row, 0.0), axis=1, keepdims=True)


def _hgrn_chunk_kernel(*refs, Lc, sub, has_state, has_alias):
    qa_ref, fa_ref, ia_ref, lb_ref = refs[:4]
    pos = 4
    s0_ref = None
    if has_state:
        s0_ref = refs[pos]
        pos += 1
    if has_alias:
        pos += 1
    o_ref, sout_ref, s_scr = refs[pos:pos + 3]
    H, K = HGRN_HEADS, HGRN_DK

    @pl.when(pl.program_id(1) == 0)
    def _():
        s_scr[...] = s0_ref[0] if has_state else jnp.zeros_like(s_scr)

    lb = lb_ref[...]
    qa = qa_ref[0]
    fa = fa_ref[0]
    q = _silu(qa)
    logf = jnp.log(lb + (1.0 - lb) * jax.nn.sigmoid(fa))
    k = (1.0 - lb) * jax.nn.sigmoid(-fa)
    vb = ia_ref[0].astype(BF16)
    G = _cumsum_rows(logf)
    qg = q * jnp.exp(G)
    nblk = Lc // sub
    worst = -G[sub - 1:sub]
    for i in range(1, nblk):
        worst = jnp.maximum(worst, G[i * sub - 1:i * sub] - G[(i + 1) * sub - 1:(i + 1) * sub])
    safe = jnp.max(worst) <= HGRN_SAFE_EXP
    row = lax.broadcasted_iota(jnp.int32, (sub, Lc), 0)
    col = lax.broadcasted_iota(jnp.int32, (sub, Lc), 1)
    tr = lax.broadcasted_iota(jnp.int32, (sub, sub), 0)
    tc = lax.broadcasted_iota(jnp.int32, (sub, sub), 1)

    def inter(h):
        sl = slice(h * K, (h + 1) * K)
        return sl, _dot(qg[:, sl].astype(BF16), s_scr[h].astype(BF16))

    @pl.when(safe)
    def _():
        for h in range(H):
            sl, o = inter(h)
            Gh, qh, kh = G[:, sl], q[:, sl], k[:, sl]
            a_rows = []
            for i in range(nblk):
                lo = i * sub
                r = Gh[lo - 1:lo] if i > 0 else jnp.zeros((1, K), F32)
                qd = (qh[lo:lo + sub] * jnp.exp(Gh[lo:lo + sub] - r)).astype(BF16)
                kd = (kh * jnp.exp(jnp.minimum(r - Gh, HGRN_SAFE_EXP))).astype(BF16)
                a_rows.append(jnp.where(col <= lo + row, _dot_nt(qd, kd), 0.0))
            a = a_rows[0] if nblk == 1 else jnp.concatenate(a_rows, axis=0)
            o_ref[0, :, sl] = o + _dot(a.astype(BF16), vb[:, sl])

    @pl.when(jnp.logical_not(safe))
    def _():
        for h in range(H):
            sl, o_int = inter(h)
            Gh, qh, kh, vh = G[:, sl], q[:, sl], k[:, sl], vb[:, sl]
            for i in range(nblk):
                lo = i * sub
                Gi, qi, ki = Gh[lo:lo + sub], qh[lo:lo + sub], kh[lo:lo + sub]
                e = jnp.exp(jnp.minimum(Gi[:, None, :] - Gi[None, :, :], 0.0))
                ad = jnp.sum(qi[:, None, :] * ki[None, :, :] * e, axis=-1)
                ad = jnp.where(tc <= tr, ad, 0.0)
                o = o_int[lo:lo + sub] + _dot(ad.astype(BF16), vh[lo:lo + sub])
                if i > 0:
                    r = Gh[lo - 1:lo]
                    qd = (qi * jnp.exp(Gi - r)).astype(BF16)
                    kd = (kh[:lo] * jnp.exp(r - Gh[:lo])).astype(BF16)
                    o = o + _dot(_dot_nt(qd, kd).astype(BF16), vh[:lo])
                o_ref[0, lo:lo + sub, sl] = o

    GL = G[Lc - 1:Lc]
    kd_all = (k * jnp.exp(GL - G)).astype(BF16)
    eGL = jnp.exp(GL)
    for h in range(H):
        sl = slice(h * K, (h + 1) * K)
        S_new = _row_to_col(eGL[:, sl]) * s_scr[h] + _dot_tn(kd_all[:, sl], vb[:, sl])
        s_scr[h] = S_new
        sout_ref[0, h] = S_new


def hgrn_scan(u3, lb, Lc, state, prev_stack, layer, n_layers):
    B, T, _ = u3.shape
    H, K = HGRN_HEADS, HGRN_DK
    W = H * K
    nc = T // Lc
    sub = min(SUB, Lc)
    st_spec = pl.BlockSpec((None, 1, H, K, K), lambda b, c: (layer, b, 0, 0, 0))
    in_specs = [pl.BlockSpec((1, Lc, W), lambda b, c: (b, c, 0)),
                pl.BlockSpec((1, Lc, W), lambda b, c: (b, c, 1)),
                pl.BlockSpec((1, Lc, W), lambda b, c: (b, c, 2)),
                pl.BlockSpec((1, W), lambda b, c: (0, 0))]
    args = [u3, u3, u3, lb]
    if state is not None:
        in_specs.append(st_spec)
        args.append(state)
    aliases = {}
    if prev_stack is not None:
        aliases = {len(args): 1}
        in_specs.append(pl.BlockSpec(memory_space=pl.ANY))
        args.append(prev_stack)
    kern = functools.partial(_hgrn_chunk_kernel, Lc=Lc, sub=sub, has_state=state is not None,
                             has_alias=prev_stack is not None)
    return pl.pallas_call(
        kern,
        out_shape=(jax.ShapeDtypeStruct((B, T, W), F32), jax.ShapeDtypeStruct((n_layers, B, H, K, K), F32)),
        grid=(B, nc),
        in_specs=in_specs,
        out_specs=(pl.BlockSpec((1, Lc, W), lambda b, c: (b, c, 0)), st_spec),
        scratch_shapes=[pltpu.VMEM((H, K, K), F32)],
        input_output_aliases=aliases,
        compiler_params=_params(("parallel", "arbitrary")),
        name="hgrn_scan",
    )(*args)


def _mlstm_chunk_kernel(*refs, Lc, pad, has_state, has_alias):
    q_ref, k_ref, v_ref, g_ref = refs[:4]
    pos = 4
    if has_state:
        c0_ref, n0_ref, m0_ref = refs[pos:pos + 3]
        pos += 3
    if has_alias:
        pos += 1
    h_ref, cout_ref, nout_ref, mout_ref, c_scr, n_scr, m_scr = refs[pos:pos + 7]
    hd = pl.program_id(1)
    ci = pl.program_id(2)

    @pl.when(ci == 0)
    def _():
        if has_state:
            c_scr[...] = c0_ref[0, 0]
            n_scr[...] = n0_ref[0, 0]
            m_scr[...] = m0_ref[0, 0]
        else:
            c_scr[...] = jnp.zeros_like(c_scr)
            n_scr[...] = jnp.zeros_like(n_scr)
            m_scr[...] = jnp.zeros_like(m_scr)

    q = q_ref[0]
    ks = k_ref[0]
    v = v_ref[0]
    g = g_ref[0]
    lane = lax.broadcasted_iota(jnp.int32, g.shape, 1)
    logi = jnp.sum(jnp.where(lane == hd, g, 0.0), axis=1, keepdims=True)
    fpre = jnp.sum(jnp.where(lane == hd + MLSTM_HEADS, g, 0.0), axis=1, keepdims=True)
    logf = jnp.minimum(fpre, 0.0) - jnp.log1p(jnp.exp(-jnp.abs(fpre)))
    t = ci * Lc + lax.broadcasted_iota(jnp.int32, (Lc, 1), 0)
    valid = t >= pad
    logf = jnp.where(valid, logf, 0.0)
    logi = jnp.where(valid, logi, NEG_BIG)
    r = lax.broadcasted_iota(jnp.int32, (Lc, Lc), 0)
    c = lax.broadcasted_iota(jnp.int32, (Lc, Lc), 1)
    eye = r == c
    tril = c <= r
    logf_row = jnp.sum(jnp.where(eye, logf, 0.0), axis=0, keepdims=True)
    logi_row = jnp.sum(jnp.where(eye, logi, 0.0), axis=0, keepdims=True)
    b_col = jnp.sum(jnp.where(tril, logf_row, 0.0), axis=1, keepdims=True)
    b_row = jnp.sum(jnp.where(c >= r, logf, 0.0), axis=0, keepdims=True)
    m0 = m_scr[:, 0:1]
    D = jnp.where(tril, b_col - b_row + logi_row, NEG_BIG)
    inter = b_col + m0
    m_t = jnp.maximum(inter, jnp.max(D, axis=1, keepdims=True))
    w_inter = jnp.exp(inter - m_t)
    qb = q.astype(BF16)
    kb = ks.astype(BF16)
    vb = v.astype(BF16)
    s = _dot_nt(qb, kb) * jnp.exp(D - m_t)
    C = c_scr[...]
    n = n_scr[...]
    num = _dot(s.astype(BF16), vb) + w_inter * _dot(qb, C.astype(BF16))
    den = jnp.sum(s, axis=1, keepdims=True) + w_inter * jnp.sum(q * n, axis=1, keepdims=True)
    h_ref[0] = num / jnp.maximum(jnp.abs(den), jnp.exp(-m_t))
    m_new = m_t[Lc - 1:Lc]
    bL = b_col[Lc - 1:Lc]
    carry = jnp.exp(bL + m0 - m_new)
    wj = jnp.exp(bL - b_col + logi - m_new)
    kw = ks * wj
    C_new = carry * C + _dot_tn(kw.astype(BF16), vb)
    n_new = carry * n + jnp.sum(kw, axis=0, keepdims=True)
    m_b = jnp.zeros((1, LANES), F32) + m_new
    c_scr[...] = C_new
    n_scr[...] = n_new
    m_scr[...] = m_b
    cout_ref[0, 0] = C_new
    nout_ref[0, 0] = n_new
    mout_ref[0, 0] = m_b


def mlstm_scan(q3, k3, v3, g3, Lc, pad, state, prev_stack, layer, n_layers):
    B, T, D = q3.shape
    H = MLSTM_HEADS
    dh = D // H
    nc = T // Lc
    seq = lambda b, h, c: (b, c, h)
    st = lambda b, h, c: (b, h, 0, 0)
    st_l = lambda b, h, c: (layer, b, h, 0, 0)
    in_specs = [pl.BlockSpec((1, Lc, dh), seq), pl.BlockSpec((1, Lc, dh), seq), pl.BlockSpec((1, Lc, dh), seq),
                pl.BlockSpec((1, Lc, LANES), lambda b, h, c: (b, c, 0))]
    args = [q3, k3, v3, g3]
    if state is not None:
        in_specs += [pl.BlockSpec((None, 1, 1, dh, dh), st_l), pl.BlockSpec((None, 1, 1, 1, dh), st_l),
                     pl.BlockSpec((None, 1, 1, 1, LANES), st_l)]
        args += list(state)
    aliases = {}
    if prev_stack is not None:
        aliases = {len(args): 1}
        in_specs.append(pl.BlockSpec(memory_space=pl.ANY))
        args.append(prev_stack)
    kern = functools.partial(_mlstm_chunk_kernel, Lc=Lc, pad=pad, has_state=state is not None,
                             has_alias=prev_stack is not None)
    return pl.pallas_call(
        kern,
        out_shape=(jax.ShapeDtypeStruct((B, T, D), F32), jax.ShapeDtypeStruct((n_layers, B, H, dh, dh), F32),
                   jax.ShapeDtypeStruct((B, H, 1, dh), F32), jax.ShapeDtypeStruct((B, H, 1, LANES), F32)),
        grid=(B, H, nc),
        in_specs=in_specs,
        out_specs=(pl.BlockSpec((1, Lc, dh), seq), pl.BlockSpec((None, 1, 1, dh, dh), st_l),
                   pl.BlockSpec((1, 1, 1, dh), st), pl.BlockSpec((1, 1, 1, LANES), st)),
        scratch_shapes=[pltpu.VMEM((dh, dh), F32), pltpu.VMEM((1, dh), F32), pltpu.VMEM((1, LANES), F32)],
        input_output_aliases=aliases,
        compiler_params=_params(("parallel", "parallel", "arbitrary")),
        name="mlstm_scan",
    )(*args)


def _attn_prompt_kernel(q_ref, k_ref, o_ref, m_scr, l_scr, acc_scr, *, tq, tk, n_keys, pad, scale):
    i = pl.program_id(1)
    rows = MLA_HEADS * tq
    q = q_ref[...].reshape(rows, QK_WIDTH)
    m_scr[...] = jnp.full((rows, LANES), NEG_BIG, F32)
    l_scr[...] = jnp.zeros((rows, LANES), F32)
    acc_scr[...] = jnp.zeros((rows, KV_LORA), F32)
    last = ((i + 1) * tq - 1) // tk

    def step(start, width, masked):
        kb = k_ref[pl.ds(start, width), :]
        s = _dot_nt(q, kb) * scale
        if masked:
            qpos = i * tq + lax.broadcasted_iota(jnp.int32, (rows, 1), 0) % tq
            kpos = start + lax.broadcasted_iota(jnp.int32, (1, width), 1)
            s = jnp.where((kpos <= qpos) & (kpos >= pad), s, NEG_BIG)
        m_old = m_scr[...]
        m_new = jnp.maximum(m_old, jnp.max(s, axis=1, keepdims=True))
        alpha = jnp.exp(m_old - m_new)
        p = jnp.concatenate([jnp.exp(s[:, c * LANES:(c + 1) * LANES] - m_new) for c in range(width // LANES)], axis=1)
        l_scr[...] = alpha * l_scr[...] + jnp.sum(p, axis=1, keepdims=True)
        pv = _dot(p.astype(BF16), kb[:, 0:KV_LORA])
        acc_scr[...] = jnp.concatenate([alpha] * (KV_LORA // LANES), axis=1) * acc_scr[...] + pv
        m_scr[...] = m_new

    n_full = n_keys // tk
    tail = n_keys - n_full * tk
    step(0, tk, True)

    def body(j, carry):
        step(pl.multiple_of(j * tk, tk), tk, False)
        return carry

    lax.fori_loop(1, last, body, 0)

    @pl.when((last > 0) & (last < n_full))
    def _():
        step(pl.multiple_of(last * tk, tk), tk, True)

    if tail:
        @pl.when(last == n_full)
        def _():
            step(n_full * tk, tail, True)

    inv = 1.0 / l_scr[...]
    out = acc_scr[...] * jnp.concatenate([inv] * (KV_LORA // LANES), axis=1)
    o_ref[...] = out.reshape(MLA_HEADS, tq, KV_LORA)


def attn_prompt(Q, K, B, Tp, pad, scale):
    tq, tk = Q_BLOCK, KV_BLOCK
    assert pad < tk <= Tp and Tp % tq == 0 and (Tp % tk) % LANES == 0
    nq = Tp // tq
    rows = MLA_HEADS * tq
    kern = functools.partial(_attn_prompt_kernel, tq=tq, tk=tk, n_keys=Tp, pad=pad, scale=scale)
    return pl.pallas_call(
        kern,
        out_shape=jax.ShapeDtypeStruct((MLA_HEADS, B * Tp, KV_LORA), F32),
        grid=(B, nq),
        in_specs=[pl.BlockSpec((MLA_HEADS, tq, QK_WIDTH), lambda b, i: (0, b * nq + i, 0)),
                  pl.BlockSpec((Tp, QK_WIDTH), lambda b, i: (b, 0))],
        out_specs=pl.BlockSpec((MLA_HEADS, tq, KV_LORA), lambda b, i: (0, b * nq + i, 0)),
        scratch_shapes=[pltpu.VMEM((rows, LANES), F32), pltpu.VMEM((rows, LANES), F32),
                        pltpu.VMEM((rows, KV_LORA), F32)],
        compiler_params=_params(("parallel", "arbitrary")),
        name="attn_prompt",
    )(Q, K)


def _attn_decode_kernel(pt_ref, q_ref, kn_ref, *rest, pages, L, scale):
    ckv_refs = rest[:pages]
    krt_refs = rest[pages:2 * pages]
    o_ref, m_scr, l_scr, acc_scr = rest[2 * pages:]
    g = pl.program_id(1)
    rows = q_ref.shape[1]

    @pl.when(g == 0)
    def _():
        m_scr[...] = jnp.full((rows, 1), NEG_BIG, F32)
        l_scr[...] = jnp.zeros((rows, 1), F32)
        acc_scr[...] = jnp.zeros((rows, KV_LORA), F32)

    q = q_ref[0]
    ql = q[:, 0:KV_LORA]
    qr = q[:, KV_LORA:KV_LORA + MLA_ROPE]

    def update(scores, vals):
        m_cur = scores[0]
        for s in scores[1:]:
            m_cur = jnp.maximum(m_cur, s)
        m_old = m_scr[...]
        m_new = jnp.maximum(m_old, jnp.max(m_cur, axis=1, keepdims=True))
        alpha = jnp.exp(m_old - m_new)
        ps = [jnp.exp(s - m_new) for s in scores]
        psum = ps[0]
        for p in ps[1:]:
            psum = psum + p
        l_scr[...] = alpha * l_scr[...] + jnp.sum(psum, axis=1, keepdims=True)
        pv = _dot(ps[0].astype(BF16), vals[0])
        for p, v in zip(ps[1:], vals[1:]):
            pv = pv + _dot(p.astype(BF16), v)
        acc_scr[...] = alpha * acc_scr[...] + pv
        m_scr[...] = m_new

    kcs = [ckv_refs[sidx][...].astype(BF16) for sidx in range(pages)]
    scores = [(_dot_nt(ql, kcs[sidx]) + _dot(qr, krt_refs[sidx][...].astype(BF16))) * scale for sidx in range(pages)]
    update(scores, kcs)

    @pl.when(g == pl.num_programs(1) - 1)
    def _():
        kn = kn_ref[0]
        s = _dot_nt(q, kn)
        tpos = lax.broadcasted_iota(jnp.int32, (rows, L), 0) % L
        jpos = lax.broadcasted_iota(jnp.int32, (rows, L), 1)
        update([jnp.where(jpos <= tpos, s * scale, NEG_BIG)], [kn[:, 0:KV_LORA]])
        o_ref[0] = acc_scr[...] / l_scr[...]


def attn_decode(Qs, Kn, cache_ckv, cache_krt, page_table, layer, scale):
    Bd, rows, _ = Qs.shape
    L = Kn.shape[1]
    n_pages = page_table.shape[1]
    page = cache_ckv.shape[2]
    P = min(PAGES_PER_STEP, n_pages)
    assert n_pages % P == 0
    kern = functools.partial(_attn_decode_kernel, pages=P, L=L, scale=scale)

    def page_spec(shape, slot):
        return pl.BlockSpec((None, None) + shape, lambda b, g, pt: (layer, pt[b, g * P + slot], 0, 0))

    grid_spec = pltpu.PrefetchScalarGridSpec(
        num_scalar_prefetch=1,
        grid=(Bd, n_pages // P),
        in_specs=[pl.BlockSpec((1, rows, QK_WIDTH), lambda b, g, pt: (b, 0, 0)),
                  pl.BlockSpec((1, L, QK_WIDTH), lambda b, g, pt: (b, 0, 0))]
                 + [page_spec((page, KV_LORA), s) for s in range(P)]
                 + [page_spec((MLA_ROPE, page), s) for s in range(P)],
        out_specs=pl.BlockSpec((1, rows, KV_LORA), lambda b, g, pt: (b, 0, 0)),
        scratch_shapes=[pltpu.VMEM((rows, 1), F32), pltpu.VMEM((rows, 1), F32), pltpu.VMEM((rows, KV_LORA), F32)],
    )
    return pl.pallas_call(
        kern,
        out_shape=jax.ShapeDtypeStruct((Bd, rows, KV_LORA), F32),
        grid_spec=grid_spec,
        compiler_params=_params(("parallel", "arbitrary")),
        name="attn_decode",
    )(page_table, Qs, Kn, *([cache_ckv] * P), *([cache_krt] * P))


def _rope_tables(pos):
    half = MLA_ROPE // 2
    inv = ROPE_THETA ** (-jnp.arange(half, dtype=F32) / half)
    ang = pos.astype(F32)[:, None] * inv[None, :]
    cos, sin = jnp.cos(ang), jnp.sin(ang)
    z = jnp.zeros((pos.shape[0], LANES - MLA_ROPE), F32)
    return jnp.concatenate([cos, cos, z], axis=1), jnp.concatenate([-sin, sin, z], axis=1)


def _expand_blockdiag(w):
    G = w.shape[0]
    per = BD_CHUNK // QKV_BLOCK
    w4 = w.reshape(G // per, per, QKV_BLOCK, QKV_BLOCK)
    eye = jnp.eye(per, dtype=w.dtype)
    dense = jnp.einsum('cgio,gh->cgiho', w4, eye)
    return dense.reshape(G // per, BD_CHUNK, BD_CHUNK).astype(BF16)


def _prep_even(e, ev_norm_w, ev_w_in, hgrn_norm_w, mla_q_norm_w, mla_w_q_b, mla_kv_norm_w, mla_w_uk, mla_w_uv, ev_w_out):
    W = HGRN_HEADS * HGRN_DK
    w = ev_w_in[e]
    o = 0
    parts = {}
    for name, width in (("q_a", W), ("f_a", W), ("i_a", W), ("g_a", W), ("cq", Q_LORA), ("ckv", KV_LORA),
                        ("kpe", MLA_ROPE), ("g_b", MLA_HEADS * MLA_V)):
        parts[name] = w[:, o:o + width]
        o += width
    zpad = jnp.zeros((w.shape[0], LANES - MLA_ROPE), w.dtype)
    w_in = jnp.concatenate([parts[n] for n in ("q_a", "f_a", "i_a", "g_a", "g_b", "cq", "ckv", "kpe")] + [zpad], axis=1)
    wqb = mla_w_q_b[e].reshape(Q_LORA, MLA_HEADS, MLA_NOPE + MLA_ROPE)
    nope = wqb[:, :, :MLA_NOPE].reshape(Q_LORA, MLA_HEADS * MLA_NOPE)
    pe = jnp.pad(wqb[:, :, MLA_NOPE:], ((0, 0), (0, 0), (0, LANES - MLA_ROPE))).reshape(Q_LORA, MLA_HEADS * LANES)
    return dict(
        nw=ev_norm_w[e][None, :], w_in=w_in.astype(BF16),
        hnw=hgrn_norm_w[e][None, :], qnw=mla_q_norm_w[e][None, :], kvnw=mla_kv_norm_w[e][None, :],
        wqb=jnp.concatenate([nope, pe], axis=1).astype(BF16),
        wuk_t=jnp.transpose(mla_w_uk[e], (1, 2, 0)).astype(BF16),
        wuv=jnp.transpose(mla_w_uv[e], (1, 0, 2)).astype(BF16),
        wout=ev_w_out[e].astype(BF16),
    )


def _prep_odd(e, od_norm_w, od_w_in, conv_w, conv_b, w_q, w_k, w_v, w_gate, b_gate, norm_w, skip, od_w_out):
    ng = w_gate.shape[2]
    return dict(
        nw=od_norm_w[e][None, :], w_in=od_w_in[e].astype(BF16),
        cw=conv_w[e], cb=conv_b[e][None, :],
        wq=_expand_blockdiag(w_q[e]), wk=_expand_blockdiag(w_k[e]), wv=_expand_blockdiag(w_v[e]),
        wg=jnp.pad(w_gate[e], ((0, 0), (0, LANES - ng))).astype(BF16),
        bg=jnp.pad(b_gate[e], (0, LANES - ng))[None, :],
        lnw=norm_w[e][None, :], skip=skip[e][None, :], wout=od_w_out[e].astype(BF16),
    )


def _even_layer(x, p, lb, cos, sin, n_seq, seq_len, Lc, state, prev_stack, layer, n_layers, attend):
    W = HGRN_HEADS * HGRN_DK
    u = norm_matmul(x, p["nw"], p["w_in"])
    Q, K, c, kr = mla_prep(u, cos, sin, p["qnw"], p["wqb"], p["wuk_t"], p["kvnw"], col0=5 * W)
    o_a, S_stack = hgrn_scan(u.reshape(n_seq, seq_len, u.shape[1]), lb, Lc, state, prev_stack, layer, n_layers)
    o_b = attend(Q, K)
    x = even_out(x, o_a.reshape(x.shape[0], W), u, o_b, p["hnw"], p["wuv"], p["wout"], ga_blk=3, gb_blk=4)
    return x, c, kr, S_stack


def _odd_layer(x, p, n_seq, seq_len, Lc, pad, state, prev_stack, layer, n_layers, conv_buf):
    u = norm_matmul(x, p["nw"], p["w_in"])
    D = u.shape[1] // 2
    dh = D // MLSTM_HEADS
    u3 = u.reshape(n_seq, seq_len, 2 * D)
    if conv_buf is None:
        shifts = None
    else:
        xp = jnp.concatenate([conv_buf, u3[:, :, :D]], axis=1)
        shifts = [xp[:, CONV_W - 1 - s: CONV_W - 1 - s + seq_len].reshape(n_seq * seq_len, D) for s in (1, 2, 3)]
    xc, q, ks, v, g = odd_mid(u, shifts, seq_len, p["cw"], p["cb"], p["wq"], p["wk"], p["wv"], p["wg"], p["bg"],
                              k_scale=dh ** -0.5)
    r3 = lambda a: a.reshape(n_seq, seq_len, a.shape[1])
    h, C_stack, n, m = mlstm_scan(r3(q), r3(ks), r3(v), r3(g), Lc, pad, state, prev_stack, layer, n_layers)
    x = odd_out(x, h.reshape(x.shape[0], D), xc, u, p["lnw"], p["skip"], p["wout"])
    new_buf = u3[:, seq_len - (CONV_W - 1):, :D]
    return x, C_stack, n[:, :, 0, :], m[:, :, 0, 0], new_buf


def kernel(x_prompt, x_sample, cache_mla_ckv, cache_mla_krope, state_hgrn, state_mlstm_C, state_mlstm_n, state_mlstm_m, state_mlstm_conv, page_table, meta_tokens, ev_norm_w, ev_w_in, hgrn_lower_bounds, hgrn_norm_w, mla_q_norm_w, mla_w_q_b, mla_kv_norm_w, mla_w_uk, mla_w_uv, ev_w_out, od_norm_w, od_w_in, mlstm_conv_w, mlstm_conv_b, mlstm_w_q, mlstm_w_k, mlstm_w_v, mlstm_w_gate, mlstm_b_gate, mlstm_norm_w, mlstm_skip, od_w_out, final_norm_w):
    B, seq, Dm = x_prompt.shape
    Bd, Ls, _ = x_sample.shape
    NE, NO = ev_norm_w.shape[0], od_norm_w.shape[0]
    depth = NE + NO
    T = seq + N_META
    Tp = -(-T // ROW_ALIGN) * ROW_ALIGN
    pad = Tp - T
    assert seq % ROW_BLOCK == 0 and (pad + N_META) % ROW_BLOCK == 0 and (Bd * Ls) % ROW_BLOCK == 0
    assert Ls >= CONV_W - 1
    past_len = page_table.shape[1] * cache_mla_ckv.shape[2]
    scale = (MLA_NOPE + MLA_ROPE) ** -0.5
    cache_krt = jnp.swapaxes(cache_mla_krope, 2, 3)
    mlstm_state = (state_mlstm_C, state_mlstm_n[:, :, :, None, :],
                   jnp.broadcast_to(state_mlstm_m[:, :, :, None, None], state_mlstm_m.shape + (1, LANES)))

    hp = jnp.concatenate([jnp.zeros((B, pad, Dm), F32), jnp.broadcast_to(meta_tokens[None], (B, N_META, Dm)), x_prompt],
                         axis=1).reshape(B * Tp, Dm)
    hs = x_sample.reshape(Bd * Ls, Dm)
    cos_p, sin_p = _rope_tables(jnp.arange(Tp) - pad)
    cos_p, sin_p = jnp.tile(cos_p, (B, 1)), jnp.tile(sin_p, (B, 1))
    cos_s, sin_s = _rope_tables(past_len + jnp.arange(Ls))
    cos_s, sin_s = jnp.tile(cos_s, (Bd, 1)), jnp.tile(sin_s, (Bd, 1))
    sm = jax.nn.softmax(hgrn_lower_bounds.astype(F32), axis=0)
    lbs = jnp.cumsum(sm, axis=0) - sm[0]

    def attend_prompt(Q, K):
        return attn_prompt(Q, K, B, Tp, pad, scale)

    def make_attend_sample(e):
        def attend(Q, K):
            Qs = Q.reshape(MLA_HEADS, Bd, Ls, QK_WIDTH).transpose(1, 0, 2, 3).reshape(Bd, MLA_HEADS * Ls, QK_WIDTH)
            o = attn_decode(Qs, K.reshape(Bd, Ls, QK_WIDTH), cache_mla_ckv, cache_krt, page_table, e, scale)
            return o.reshape(Bd, MLA_HEADS, Ls, KV_LORA).transpose(1, 0, 2, 3).reshape(MLA_HEADS, Bd * Ls, KV_LORA)
        return attend

    outs_p = {k: [] for k in ("ckv", "kr", "n", "m", "cv")}
    outs_s = {k: [] for k in ("ckv", "kr", "n", "m", "cv")}
    S_p = S_s = C_p = C_s = None
    for l in range(depth):
        e = l // 2
        if l % 2 == 0:
            p = _prep_even(e, ev_norm_w, ev_w_in, hgrn_norm_w, mla_q_norm_w, mla_w_q_b, mla_kv_norm_w, mla_w_uk,
                           mla_w_uv, ev_w_out)
            lb = lbs[e][None, :]
            hp, c, kr, S_p = _even_layer(hp, p, lb, cos_p, sin_p, B, Tp, HGRN_CHUNK, None, S_p, e, NE, attend_prompt)
            outs_p["ckv"].append(c.reshape(B, Tp, KV_LORA)[:, pad:])
            outs_p["kr"].append(kr.reshape(B, Tp, MLA_ROPE)[:, pad:])
            hs, c, kr, S_s = _even_layer(hs, p, lb, cos_s, sin_s, Bd, Ls, Ls, state_hgrn, S_s, e, NE,
                                         make_attend_sample(e))
            outs_s["ckv"].append(c.reshape(Bd, Ls, KV_LORA))
            outs_s["kr"].append(kr.reshape(Bd, Ls, MLA_ROPE))
        else:
            p = _prep_odd(e, od_norm_w, od_w_in, mlstm_conv_w, mlstm_conv_b, mlstm_w_q, mlstm_w_k, mlstm_w_v,
                          mlstm_w_gate, mlstm_b_gate, mlstm_norm_w, mlstm_skip, od_w_out)
            hp, C_p, n, m, cv = _odd_layer(hp, p, B, Tp, MLSTM_CHUNK, pad, None, C_p, e, NO, None)
            for key, val in zip(("n", "m", "cv"), (n, m, cv)):
                outs_p[key].append(val)
            hs, C_s, n, m, cv = _odd_layer(hs, p, Bd, Ls, Ls, 0, mlstm_state, C_s, e, NO, state_mlstm_conv[e])
            for key, val in zip(("n", "m", "cv"), (n, m, cv)):
                outs_s[key].append(val)

    fw = final_norm_w[None, :]
    blocks_in = Tp // ROW_BLOCK
    skip_blocks = (pad + N_META) // ROW_BLOCK
    y_prompt = final_norm(hp, fw, skip_blocks, blocks_in, blocks_in - skip_blocks).reshape(B, seq, Dm)
    y_sample = final_norm(hs, fw, 0, (Bd * Ls) // ROW_BLOCK, (Bd * Ls) // ROW_BLOCK).reshape(Bd, Ls, Dm)
    st = lambda xs: jnp.stack(xs)
    return (y_prompt, y_sample,
            st(outs_p["ckv"]), st(outs_p["kr"]), S_p, C_p, st(outs_p["n"]), st(outs_p["m"]), st(outs_p["cv"]),
            st(outs_s["ckv"]), st(outs_s["kr"]), S_s, C_s, st(outs_s["n"]), st(outs_s["m"]), st(outs_s["cv"]))
```

```python
import functools

import jax
import jax.numpy as jnp
from jax import lax
from jax.experimental import pallas as pl
from jax.experimental.pallas import tpu as pltpu

F32 = jnp.float32
BF16 = jnp.bfloat16

N_META = 16
HGRN_HEADS = 8
HGRN_DK = 128
MLA_HEADS = 8
MLA_NOPE = 128
MLA_ROPE = 64
MLA_V = 128
Q_LORA = 512
KV_LORA = 256
ROPE_THETA = 10000.0
MLSTM_HEADS = 4
QKV_BLOCK = 4
CONV_W = 4
NEG_BIG = -1e30
RMS_EPS = 1e-6
LN_EPS = 1e-5
LOG2E = 1.4426950408889634

LANES = 128
ROW_ALIGN = 256
ROW_BLOCK = 256
QK_WIDTH = KV_LORA + LANES
Q_BLOCK = 128
KV_BLOCK = 512
ATTN_ROW_SPLITS = 2
HGRN_CHUNK = 128
MLSTM_CHUNK = 256
SUB = 16
PAGES_PER_STEP = 16
HGRN_SAFE_EXP = 60.0
BD_CHUNK = 256
VMEM_LIMIT = 56 * 1024 * 1024

_NT = (((1,), (1,)), ((), ()))
_TN = (((0,), (0,)), ((), ()))


def _dot(a, b):
    return jnp.dot(a, b, preferred_element_type=F32)


def _dot_nt(a, b):
    return lax.dot_general(a, b, _NT, preferred_element_type=F32)


def _dot_tn(a, b):
    return lax.dot_general(a, b, _TN, preferred_element_type=F32)


def _params(sem):
    return pltpu.CompilerParams(dimension_semantics=sem, vmem_limit_bytes=VMEM_LIMIT)


def _rms(x, w, eps=RMS_EPS):
    return x * lax.rsqrt(jnp.mean(x * x, axis=-1, keepdims=True) + eps) * w


def _silu(x):
    return x * jax.nn.sigmoid(x)


def _const_spec(shape):
    nd = len(shape)
    return pl.BlockSpec(shape, lambda *_: (0,) * nd, pipeline_mode=pl.Buffered(1))


def _norm_matmul_kernel(x_ref, nw_ref, w_ref, o_ref, *, chunks):
    xn = _rms(x_ref[...], nw_ref[...]).astype(BF16)
    for start, width in chunks:
        o_ref[:, start:start + width] = _dot(xn, w_ref[:, start:start + width])


def norm_matmul(x, nw, w):
    R, D = x.shape
    N = w.shape[1]
    tm = ROW_BLOCK
    chunks, s = [], 0
    while s < N:
        width = min(1024, N - s)
        chunks.append((s, width))
        s += width
    return pl.pallas_call(
        functools.partial(_norm_matmul_kernel, chunks=tuple(chunks)),
        out_shape=jax.ShapeDtypeStruct((R, N), F32),
        grid=(R // tm,),
        in_specs=[pl.BlockSpec((tm, D), lambda i: (i, 0)), _const_spec((1, D)), _const_spec((D, N))],
        out_specs=pl.BlockSpec((tm, N), lambda i: (i, 0)),
        compiler_params=_params(("parallel",)),
        name="norm_matmul",
    )(x, nw, w)


def _rope_tile(x, cos, sin):
    lane = lax.broadcasted_iota(jnp.int32, x.shape, 1)
    half = MLA_ROPE // 2
    partner = jnp.where(lane % MLA_ROPE < half, pltpu.roll(x, LANES - half, 1), pltpu.roll(x, half, 1))
    return x * cos + partner * sin


def _mla_prep_kernel(cq_ref, ckv_ref, kpe_ref, cos_ref, sin_ref, qnw_ref, wqb_ref, wuk_ref, kvnw_ref,
                     q_ref, k_ref, kt_ref, c_ref, kr_ref):
    cos = cos_ref[...]
    sin = sin_ref[...]
    cqn = _rms(cq_ref[...], qnw_ref[...]).astype(BF16)
    qb = _dot(cqn, wqb_ref[...])
    nope_w = MLA_HEADS * MLA_NOPE
    for h in range(MLA_HEADS):
        qn = qb[:, h * MLA_NOPE:(h + 1) * MLA_NOPE].astype(BF16)
        q_ref[h, :, 0:KV_LORA] = _dot(qn, wuk_ref[h]).astype(BF16)
        qpe = qb[:, nope_w + h * LANES: nope_w + (h + 1) * LANES]
        q_ref[h, :, KV_LORA:QK_WIDTH] = _rope_tile(qpe, cos, sin).astype(BF16)
    c = _rms(ckv_ref[...], kvnw_ref[...])
    kr = _rope_tile(kpe_ref[...], cos, sin)
    c_ref[...] = c
    kr_ref[...] = kr[:, 0:MLA_ROPE]
    k_ref[:, 0:KV_LORA] = c.astype(BF16)
    k_ref[:, KV_LORA:QK_WIDTH] = kr.astype(BF16)
    kt_ref[0:KV_LORA, :] = c.T.astype(BF16)
    kt_ref[KV_LORA:QK_WIDTH, :] = kr.T.astype(BF16)


def mla_prep(u, cos, sin, qnw, wqb, wuk_t, kvnw, col0):
    R = u.shape[0]
    tm = ROW_BLOCK
    cq_blk = col0 // Q_LORA
    ckv_blk = (col0 + Q_LORA) // KV_LORA
    kpe_blk = (col0 + Q_LORA + KV_LORA) // LANES
    return pl.pallas_call(
        _mla_prep_kernel,
        out_shape=(jax.ShapeDtypeStruct((MLA_HEADS, R, QK_WIDTH), BF16),
                   jax.ShapeDtypeStruct((R, QK_WIDTH), BF16),
                   jax.ShapeDtypeStruct((QK_WIDTH, R), BF16),
                   jax.ShapeDtypeStruct((R, KV_LORA), F32),
                   jax.ShapeDtypeStruct((R, MLA_ROPE), F32)),
        grid=(R // tm,),
        in_specs=[pl.BlockSpec((tm, Q_LORA), lambda i: (i, cq_blk)),
                  pl.BlockSpec((tm, KV_LORA), lambda i: (i, ckv_blk)),
                  pl.BlockSpec((tm, LANES), lambda i: (i, kpe_blk)),
                  pl.BlockSpec((tm, LANES), lambda i: (i, 0)),
                  pl.BlockSpec((tm, LANES), lambda i: (i, 0)),
                  _const_spec(qnw.shape), _const_spec(wqb.shape), _const_spec(wuk_t.shape), _const_spec(kvnw.shape)],
        out_specs=(pl.BlockSpec((MLA_HEADS, tm, QK_WIDTH), lambda i: (0, i, 0)),
                   pl.BlockSpec((tm, QK_WIDTH), lambda i: (i, 0)),
                   pl.BlockSpec((QK_WIDTH, tm), lambda i: (0, i)),
                   pl.BlockSpec((tm, KV_LORA), lambda i: (i, 0)),
                   pl.BlockSpec((tm, MLA_ROPE), lambda i: (i, 0))),
        compiler_params=_params(("parallel",)),
        name="mla_prep",
    )(u, u, u, cos, sin, qnw, wqb, wuk_t, kvnw)


def _even_out_kernel(x_ref, oa_ref, ga_ref, gb_ref, ob_ref, hnw_ref, wuv_ref, wout_ref, o_ref, y_scr):
    hw = HGRN_HEADS * HGRN_DK
    for h in range(HGRN_HEADS):
        sl = slice(h * HGRN_DK, (h + 1) * HGRN_DK)
        ya = _rms(oa_ref[:, sl], hnw_ref[:, sl])
        y_scr[:, sl] = (ya * _silu(ga_ref[:, sl])).astype(BF16)
    for h in range(MLA_HEADS):
        sl = slice(h * MLA_V, (h + 1) * MLA_V)
        yb = _dot(ob_ref[h].astype(BF16), wuv_ref[h])
        y_scr[:, hw + h * MLA_V: hw + (h + 1) * MLA_V] = (yb * _silu(gb_ref[:, sl])).astype(BF16)
    o_ref[...] = x_ref[...] + _dot(y_scr[...], wout_ref[...])


def even_out(x, o_a, u, o_b, hnw, wuv, wout, ga_blk, gb_blk):
    R, D = x.shape
    tm = ROW_BLOCK
    W = HGRN_HEADS * HGRN_DK
    return pl.pallas_call(
        _even_out_kernel,
        out_shape=jax.ShapeDtypeStruct((R, D), F32),
        grid=(R // tm,),
        in_specs=[pl.BlockSpec((tm, D), lambda i: (i, 0)),
                  pl.BlockSpec((tm, W), lambda i: (i, 0)),
                  pl.BlockSpec((tm, W), lambda i: (i, ga_blk)),
                  pl.BlockSpec((tm, W), lambda i: (i, gb_blk)),
                  pl.BlockSpec((MLA_HEADS, tm, KV_LORA), lambda i: (0, i, 0)),
                  _const_spec(hnw.shape), _const_spec(wuv.shape), _const_spec(wout.shape)],
        out_specs=pl.BlockSpec((tm, D), lambda i: (i, 0)),
        scratch_shapes=[pltpu.VMEM((tm, 2 * W), BF16)],
        compiler_params=_params(("parallel",)),
        name="even_out",
    )(x, o_a, u, u, o_b, hnw, wuv, wout)


def _odd_mid_kernel(*refs, tm, blocks_per_seq, shifted_inputs, k_scale):
    if shifted_inputs:
        x_ref, x1_ref, x2_ref, x3_ref = refs[:4]
        rest = refs[4:]
    else:
        x_ref, halo_ref = refs[:2]
        rest = refs[2:]
    cw_ref, cb_ref, wq_ref, wk_ref, wv_ref, wg_ref, bg_ref, xc_ref, q_ref, k_ref, v_ref, g_ref = rest[:12]
    x = x_ref[...]
    if shifted_inputs:
        shifted = [x1_ref[...], x2_ref[...], x3_ref[...]]
    else:
        xs = rest[12]
        first = (pl.program_id(0) % blocks_per_seq) == 0
        halo = halo_ref[...]
        xs[0:8, :] = jnp.where(first, jnp.zeros_like(halo), halo)
        xs[8:, :] = x
        shifted = [xs[pl.ds(8 - s, tm), :] for s in (1, 2, 3)]
    cw = cw_ref[...]
    pre = cb_ref[...] + x * cw[3:4] + shifted[0] * cw[2:3] + shifted[1] * cw[1:2] + shifted[2] * cw[0:1]
    xc = _silu(pre)
    xc_ref[...] = xc
    D = x.shape[1]
    g = jnp.zeros((x.shape[0], LANES), F32) + bg_ref[...]
    for c in range(D // BD_CHUNK):
        sl = slice(c * BD_CHUNK, (c + 1) * BD_CHUNK)
        xcb = xc[:, sl].astype(BF16)
        q = _dot(xcb, wq_ref[c])
        k = _dot(xcb, wk_ref[c])
        v = _dot(x[:, sl].astype(BF16), wv_ref[c])
        q_ref[:, sl] = q
        k_ref[:, sl] = k * k_scale
        v_ref[:, sl] = v
        g = g + _dot(q.astype(BF16), wg_ref[c * BD_CHUNK:(c + 1) * BD_CHUNK, :])
        g = g + _dot(k.astype(BF16), wg_ref[D + c * BD_CHUNK: D + (c + 1) * BD_CHUNK, :])
        g = g + _dot(v.astype(BF16), wg_ref[2 * D + c * BD_CHUNK: 2 * D + (c + 1) * BD_CHUNK, :])
    g_ref[...] = g


def odd_mid(u, shifts, rows_per_seq, cw, cb, wq, wk, wv, wg, bg, k_scale):
    R = u.shape[0]
    D = u.shape[1] // 2
    tm = ROW_BLOCK
    row = lambda i: (i, 0)
    wspecs = [_const_spec(a.shape) for a in (cw, cb, wq, wk, wv, wg, bg)]
    outs = tuple(jax.ShapeDtypeStruct((R, D), F32) for _ in range(4)) + (jax.ShapeDtypeStruct((R, LANES), F32),)
    out_specs = tuple(pl.BlockSpec((tm, D), row) for _ in range(4)) + (pl.BlockSpec((tm, LANES), row),)
    if shifts is None:
        ins = [pl.BlockSpec((tm, D), row),
               pl.BlockSpec((8, D), lambda i: (jnp.maximum(i * (tm // 8) - 1, 0), 0))]
        args = (u, u)
        scratch = [pltpu.VMEM((tm + 8, D), F32)]
    else:
        ins = [pl.BlockSpec((tm, D), row)] * 4
        args = (u,) + tuple(shifts)
        scratch = []
    kern = functools.partial(_odd_mid_kernel, tm=tm, blocks_per_seq=rows_per_seq // tm,
                             shifted_inputs=shifts is not None, k_scale=k_scale)
    return pl.pallas_call(
        kern, out_shape=outs, grid=(R // tm,), in_specs=ins + wspecs, out_specs=out_specs,
        scratch_shapes=scratch, compiler_params=_params(("parallel",)), name="odd_mid",
    )(*args, cw, cb, wq, wk, wv, wg, bg)


def _odd_out_kernel(x_ref, h_ref, xc_ref, z_ref, nw_ref, skip_ref, wout_ref, o_ref, y_scr, *, dh):
    D = h_ref.shape[1]
    for hd in range(D // dh):
        sl = slice(hd * dh, (hd + 1) * dh)
        h = h_ref[:, sl]
        mu = jnp.mean(h, axis=-1, keepdims=True)
        hc = h - mu
        var = jnp.mean(hc * hc, axis=-1, keepdims=True)
        hn = hc * lax.rsqrt(var + LN_EPS) * nw_ref[:, sl]
        y = (hn + skip_ref[:, sl] * xc_ref[:, sl]) * _silu(z_ref[:, sl])
        y_scr[:, sl] = y.astype(BF16)
    o_ref[...] = x_ref[...] + _dot(y_scr[...], wout_ref[...])


def odd_out(x, h, xc, u, nw, skip, wout):
    R, Dm = x.shape
    D = h.shape[1]
    tm = ROW_BLOCK
    row = lambda i: (i, 0)
    return pl.pallas_call(
        functools.partial(_odd_out_kernel, dh=D // MLSTM_HEADS),
        out_shape=jax.ShapeDtypeStruct((R, Dm), F32),
        grid=(R // tm,),
        in_specs=[pl.BlockSpec((tm, Dm), row), pl.BlockSpec((tm, D), row), pl.BlockSpec((tm, D), row),
                  pl.BlockSpec((tm, D), lambda i: (i, 1)),
                  _const_spec(nw.shape), _const_spec(skip.shape), _const_spec(wout.shape)],
        out_specs=pl.BlockSpec((tm, Dm), row),
        scratch_shapes=[pltpu.VMEM((tm, D), BF16)],
        compiler_params=_params(("parallel",)),
        name="odd_out",
    )(x, h, xc, u, nw, skip, wout)


def _final_norm_kernel(x_ref, w_ref, o_ref):
    o_ref[...] = _rms(x_ref[...], w_ref[...])


def final_norm(x, w, skip_blocks, blocks_per_seq_in, blocks_per_seq_out):
    R, D = x.shape
    tm = ROW_BLOCK
    n_seq = R // (blocks_per_seq_in * tm)
    Ro = n_seq * blocks_per_seq_out * tm

    def in_map(i):
        return ((i // blocks_per_seq_out) * blocks_per_seq_in + skip_blocks + i % blocks_per_seq_out, 0)

    return pl.pallas_call(
        _final_norm_kernel,
        out_shape=jax.ShapeDtypeStruct((Ro, D), F32),
        grid=(Ro // tm,),
        in_specs=[pl.BlockSpec((tm, D), in_map), _const_spec(w.shape)],
        out_specs=pl.BlockSpec((tm, D), lambda i: (i, 0)),
        compiler_params=_params(("parallel",)),
        name="final_norm",
    )(x, w)


def _split3(x):
    hi = x.astype(BF16)
    r1 = x - hi.astype(F32)
    mid = r1.astype(BF16)
    lo = (r1 - mid.astype(F32)).astype(BF16)
    return hi, mid, lo


def _cumsum_rows(x):
    L = x.shape[0]
    r = lax.broadcasted_iota(jnp.int32, (L, L), 0)
    c = lax.broadcasted_iota(jnp.int32, (L, L), 1)
    tri = (c <= r).astype(BF16)
    hi, mid, lo = _split3(x)
    return _dot(tri, hi) + _dot(tri, mid) + _dot(tri, lo)


def _row_to_col(row):
    n = row.shape[1]
    r = lax.broadcasted_iota(jnp.int32, (n, n), 0)
    c = lax.broadcasted_iota(jnp.int32, (n, n), 1)
    return jnp.sum(jnp.where(r == c, row, 0.0), axis=1, keepdims=True)


def _hgrn_chunk_kernel(*refs, Lc, sub, has_state, has_alias):
    qa_ref, fa_ref, ia_ref, lb_ref = refs[:4]
    pos = 4
    s0_ref = None
    if has_state:
        s0_ref = refs[pos]
        pos += 1
    if has_alias:
        pos += 1
    o_ref, sout_ref, s_scr = refs[pos:pos + 3]
    H, K = HGRN_HEADS, HGRN_DK

    @pl.when(pl.program_id(1) == 0)
    def _():
        s_scr[...] = s0_ref[0] if has_state else jnp.zeros_like(s_scr)

    lb = lb_ref[...]
    qa = qa_ref[0]
    fa = fa_ref[0]
    q = _silu(qa)
    logf = jnp.log(lb + (1.0 - lb) * jax.nn.sigmoid(fa))
    k = (1.0 - lb) * jax.nn.sigmoid(-fa)
    vb = ia_ref[0].astype(BF16)
    G = _cumsum_rows(logf)
    qg = q * jnp.exp(G)
    nblk = Lc // sub
    worst = -G[sub - 1:sub]
    for i in range(1, nblk):
        worst = jnp.maximum(worst, G[i * sub - 1:i * sub] - G[(i + 1) * sub - 1:(i + 1) * sub])
    safe = jnp.max(worst) <= HGRN_SAFE_EXP
    row = lax.broadcasted_iota(jnp.int32, (sub, Lc), 0)
    col = lax.broadcasted_iota(jnp.int32, (sub, Lc), 1)
    tr = lax.broadcasted_iota(jnp.int32, (sub, sub), 0)
    tc = lax.broadcasted_iota(jnp.int32, (sub, sub), 1)

    def inter(h):
        sl = slice(h * K, (h + 1) * K)
        return sl, _dot(qg[:, sl].astype(BF16), s_scr[h].astype(BF16))

    @pl.when(safe)
    def _():
        for h in range(H):
            sl, o = inter(h)
            Gh, qh, kh = G[:, sl], q[:, sl], k[:, sl]
            a_rows = []
            for i in range(nblk):
                lo = i * sub
                r = Gh[lo - 1:lo] if i > 0 else jnp.zeros((1, K), F32)
                qd = (qh[lo:lo + sub] * jnp.exp(Gh[lo:lo + sub] - r)).astype(BF16)
                kd = (kh * jnp.exp(jnp.minimum(r - Gh, HGRN_SAFE_EXP))).astype(BF16)
                a_rows.append(jnp.where(col <= lo + row, _dot_nt(qd, kd), 0.0))
            a = a_rows[0] if nblk == 1 else jnp.concatenate(a_rows, axis=0)
            o_ref[0, :, sl] = o + _dot(a.astype(BF16), vb[:, sl])

    @pl.when(jnp.logical_not(safe))
    def _():
        for h in range(H):
            sl, o_int = inter(h)
            Gh, qh, kh, vh = G[:, sl], q[:, sl], k[:, sl], vb[:, sl]
            for i in range(nblk):
                lo = i * sub
                Gi, qi, ki = Gh[lo:lo + sub], qh[lo:lo + sub], kh[lo:lo + sub]
                e = jnp.exp(jnp.minimum(Gi[:, None, :] - Gi[None, :, :], 0.0))
                ad = jnp.sum(qi[:, None, :] * ki[None, :, :] * e, axis=-1)
                ad = jnp.where(tc <= tr, ad, 0.0)
                o = o_int[lo:lo + sub] + _dot(ad.astype(BF16), vh[lo:lo + sub])
                if i > 0:
                    r = Gh[lo - 1:lo]
                    qd = (qi * jnp.exp(Gi - r)).astype(BF16)
                    kd = (kh[:lo] * jnp.exp(r - Gh[:lo])).astype(BF16)
                    o = o + _dot(_dot_nt(qd, kd).astype(BF16), vh[:lo])
                o_ref[0, lo:lo + sub, sl] = o

    GL = G[Lc - 1:Lc]
    kd_all = (k * jnp.exp(GL - G)).astype(BF16)
    eGL = jnp.exp(GL)
    for h in range(H):
        sl = slice(h * K, (h + 1) * K)
        S_new = _row_to_col(eGL[:, sl]) * s_scr[h] + _dot_tn(kd_all[:, sl], vb[:, sl])
        s_scr[h] = S_new
        sout_ref[0, h] = S_new


def hgrn_scan(u3, lb, Lc, state, prev_stack, layer, n_layers):
    B, T, _ = u3.shape
    H, K = HGRN_HEADS, HGRN_DK
    W = H * K
    nc = T // Lc
    sub = min(SUB, Lc)
    st_spec = pl.BlockSpec((None, 1, H, K, K), lambda b, c: (layer, b, 0, 0, 0))
    in_specs = [pl.BlockSpec((1, Lc, W), lambda b, c: (b, c, 0)),
                pl.BlockSpec((1, Lc, W), lambda b, c: (b, c, 1)),
                pl.BlockSpec((1, Lc, W), lambda b, c: (b, c, 2)),
                pl.BlockSpec((1, W), lambda b, c: (0, 0))]
    args = [u3, u3, u3, lb]
    if state is not None:
        in_specs.append(st_spec)
        args.append(state)
    aliases = {}
    if prev_stack is not None:
        aliases = {len(args): 1}
        in_specs.append(pl.BlockSpec(memory_space=pl.ANY))
        args.append(prev_stack)
    kern = functools.partial(_hgrn_chunk_kernel, Lc=Lc, sub=sub, has_state=state is not None,
                             has_alias=prev_stack is not None)
    return pl.pallas_call(
        kern,
        out_shape=(jax.ShapeDtypeStruct((B, T, W), F32), jax.ShapeDtypeStruct((n_layers, B, H, K, K), F32)),
        grid=(B, nc),
        in_specs=in_specs,
        out_specs=(pl.BlockSpec((1, Lc, W), lambda b, c: (b, c, 0)), st_spec),
        scratch_shapes=[pltpu.VMEM((H, K, K), F32)],
        input_output_aliases=aliases,
        compiler_params=_params(("parallel", "arbitrary")),
        name="hgrn_scan",
    )(*args)


def _mlstm_chunk_kernel(*refs, Lc, pad, has_state, has_alias, single_chunk):
    q_ref, k_ref, v_ref, g_ref = refs[:4]
    pos = 4
    if has_state:
        c0_ref, n0_ref, m0_ref = refs[pos:pos + 3]
        pos += 3
    if has_alias:
        pos += 1
    h_ref, cout_ref, nout_ref, mout_ref = refs[pos:pos + 4]
    ci = pl.program_id(1)
    H = MLSTM_HEADS
    dh = q_ref.shape[2] // H
    carried = not single_chunk

    if carried:
        @pl.when(ci == 0)
        def _():
            if has_state:
                cout_ref[...] = c0_ref[...]
                nout_ref[...] = n0_ref[...]
                mout_ref[...] = m0_ref[...]
            else:
                cout_ref[...] = jnp.zeros_like(cout_ref)
                nout_ref[...] = jnp.zeros_like(nout_ref)
                mout_ref[...] = jnp.zeros_like(mout_ref)

    g = g_ref[0]
    t = ci * Lc + lax.broadcasted_iota(jnp.int32, (Lc, 1), 0)
    valid = t >= pad
    r = lax.broadcasted_iota(jnp.int32, (Lc, Lc), 0)
    c = lax.broadcasted_iota(jnp.int32, (Lc, Lc), 1)
    eye = r == c
    tril = c <= r
    for hd in range(H):
        sl = slice(hd * dh, (hd + 1) * dh)
        q = q_ref[0, :, sl]
        ks = k_ref[0, :, sl]
        vb = v_ref[0, :, sl].astype(BF16)
        logi = g[:, hd:hd + 1]
        fpre = g[:, H + hd:H + hd + 1]
        logf = jnp.minimum(fpre, 0.0) - jnp.log1p(jnp.exp(-jnp.abs(fpre)))
        logf = jnp.where(valid, logf, 0.0)
        logi = jnp.where(valid, logi, NEG_BIG)
        logf_row = jnp.sum(jnp.where(eye, logf, 0.0), axis=0, keepdims=True)
        logi_row = jnp.sum(jnp.where(eye, logi, 0.0), axis=0, keepdims=True)
        b_col = jnp.sum(jnp.where(tril, logf_row, 0.0), axis=1, keepdims=True)
        b_row = jnp.sum(jnp.where(c >= r, logf, 0.0), axis=0, keepdims=True)
        if carried:
            C, n, m0 = cout_ref[0, hd], nout_ref[0, hd], mout_ref[0, hd][:, 0:1]
        elif has_state:
            C, n, m0 = c0_ref[0, hd], n0_ref[0, hd], m0_ref[0, hd][:, 0:1]
        else:
            C, n, m0 = jnp.zeros((dh, dh), F32), jnp.zeros((1, dh), F32), jnp.zeros((1, 1), F32)
        D = jnp.where(tril, b_col - b_row + logi_row, NEG_BIG)
        inter = b_col + m0
        m_t = jnp.maximum(inter, jnp.max(D, axis=1, keepdims=True))
        w_inter = jnp.exp(inter - m_t)
        qb = q.astype(BF16)
        s = _dot_nt(qb, ks.astype(BF16)) * jnp.exp(D - m_t)
        num = _dot(s.astype(BF16), vb) + w_inter * _dot(qb, C.astype(BF16))
        den = jnp.sum(s, axis=1, keepdims=True) + w_inter * jnp.sum(q * n, axis=1, keepdims=True)
        h_ref[0, :, sl] = num / jnp.maximum(jnp.abs(den), jnp.exp(-m_t))
        m_new = m_t[Lc - 1:Lc]
        bL = b_col[Lc - 1:Lc]
        carry = jnp.exp(bL + m0 - m_new)
        wj = jnp.exp(bL - b_col + logi - m_new)
        kw = ks * wj
        C_new = carry * C + _dot_tn(kw.astype(BF16), vb)
        n_new = carry * n + jnp.sum(kw, axis=0, keepdims=True)
        m_b = jnp.zeros((1, LANES), F32) + m_new
        cout_ref[0, hd] = C_new
        nout_ref[0, hd] = n_new
        mout_ref[0, hd] = m_b


def mlstm_scan(q3, k3, v3, g3, Lc, pad, state, prev_stack, layer, n_layers):
    B, T, D = q3.shape
    H = MLSTM_HEADS
    dh = D // H
    nc = T // Lc
    seq = lambda b, c: (b, c, 0)
    st = lambda b, c: (b, 0, 0, 0)
    st_l = lambda b, c: (layer, b, 0, 0, 0)
    in_specs = [pl.BlockSpec((1, Lc, D), seq), pl.BlockSpec((1, Lc, D), seq), pl.BlockSpec((1, Lc, D), seq),
                pl.BlockSpec((1, Lc, LANES), seq)]
    args = [q3, k3, v3, g3]
    if state is not None:
        in_specs += [pl.BlockSpec((None, 1, H, dh, dh), st_l), pl.BlockSpec((None, 1, H, 1, dh), st_l),
                     pl.BlockSpec((None, 1, H, 1, LANES), st_l)]
        args += list(state)
    aliases = {}
    if prev_stack is not None:
        aliases = {len(args): 1}
        in_specs.append(pl.BlockSpec(memory_space=pl.ANY))
        args.append(prev_stack)
    kern = functools.partial(_mlstm_chunk_kernel, Lc=Lc, pad=pad, has_state=state is not None,
                             has_alias=prev_stack is not None, single_chunk=nc == 1)
    return pl.pallas_call(
        kern,
        out_shape=(jax.ShapeDtypeStruct((B, T, D), F32), jax.ShapeDtypeStruct((n_layers, B, H, dh, dh), F32),
                   jax.ShapeDtypeStruct((B, H, 1, dh), F32), jax.ShapeDtypeStruct((B, H, 1, LANES), F32)),
        grid=(B, nc),
        in_specs=in_specs,
        out_specs=(pl.BlockSpec((1, Lc, D), seq), pl.BlockSpec((None, 1, H, dh, dh), st_l),
                   pl.BlockSpec((1, H, 1, dh), st), pl.BlockSpec((1, H, 1, LANES), st)),
        input_output_aliases=aliases,
        compiler_params=_params(("parallel", "arbitrary")),
        name="mlstm_scan",
    )(*args)


def _attn_prompt_kernel(q_ref, k_ref, kt_ref, o_ref, m_scr, l_scr, acc_scr, *, tq, tk, n_keys, pad, scale):
    i = pl.program_id(1)
    rows = MLA_HEADS * tq
    q = q_ref[...].reshape(rows, QK_WIDTH)
    m_scr[...] = jnp.full((rows, LANES), NEG_BIG, F32)
    l_scr[...] = jnp.zeros((rows, LANES), F32)
    acc_scr[...] = jnp.zeros((rows, KV_LORA), F32)
    last = ((i + 1) * tq - 1) // tk

    rsz = rows // ATTN_ROW_SPLITS

    scale2 = scale * LOG2E

    def step(start, width, masked):
        ktb = kt_ref[:, pl.ds(start, width)]
        vb = k_ref[pl.ds(start, width), 0:KV_LORA]
        for r in range(ATTN_ROW_SPLITS):
            rs = slice(r * rsz, (r + 1) * rsz)
            s = _dot(q[rs], ktb) * scale2
            if masked:
                qpos = i * tq + (r * rsz + lax.broadcasted_iota(jnp.int32, (rsz, 1), 0)) % tq
                kpos = start + lax.broadcasted_iota(jnp.int32, (1, width), 1)
                s = jnp.where((kpos <= qpos) & (kpos >= pad), s, NEG_BIG)
            m_old = m_scr[rs, :]
            m_new = jnp.maximum(m_old, jnp.max(s, axis=1, keepdims=True))
            alpha = jnp.exp2(m_old - m_new)
            p = jnp.concatenate([jnp.exp2(s[:, c * LANES:(c + 1) * LANES] - m_new) for c in range(width // LANES)],
                                axis=1)
            l_scr[rs, :] = alpha * l_scr[rs, :] + jnp.sum(p, axis=1, keepdims=True)
            pv = _dot(p.astype(BF16), vb)
            acc_scr[rs, :] = jnp.concatenate([alpha] * (KV_LORA // LANES), axis=1) * acc_scr[rs, :] + pv
            m_scr[rs, :] = m_new

    n_full = n_keys // tk
    tail = n_keys - n_full * tk
    step(0, tk, True)

    def body(j, carry):
        step(pl.multiple_of(j * tk, tk), tk, False)
        return carry

    lax.fori_loop(1, last, body, 0)

    @pl.when((last > 0) & (last < n_full))
    def _():
        step(pl.multiple_of(last * tk, tk), tk, True)

    if tail:
        @pl.when(last == n_full)
        def _():
            step(n_full * tk, tail, True)

    inv = 1.0 / l_scr[...]
    out = acc_scr[...] * jnp.concatenate([inv] * (KV_LORA // LANES), axis=1)
    o_ref[...] = out.reshape(MLA_HEADS, tq, KV_LORA)


def attn_prompt(Q, K, KT, B, Tp, pad, scale):
    tq, tk = Q_BLOCK, KV_BLOCK
    assert pad < tk <= Tp and Tp % tq == 0 and (Tp % tk) % LANES == 0
    nq = Tp // tq
    rows = MLA_HEADS * tq
    kern = functools.partial(_attn_prompt_kernel, tq=tq, tk=tk, n_keys=Tp, pad=pad, scale=scale)
    return pl.pallas_call(
        kern,
        out_shape=jax.ShapeDtypeStruct((MLA_HEADS, B * Tp, KV_LORA), F32),
        grid=(B, nq),
        in_specs=[pl.BlockSpec((MLA_HEADS, tq, QK_WIDTH), lambda b, i: (0, b * nq + i, 0)),
                  pl.BlockSpec((Tp, QK_WIDTH), lambda b, i: (b, 0)),
                  pl.BlockSpec((QK_WIDTH, Tp), lambda b, i: (0, b))],
        out_specs=pl.BlockSpec((MLA_HEADS, tq, KV_LORA), lambda b, i: (0, b * nq + i, 0)),
        scratch_shapes=[pltpu.VMEM((rows, LANES), F32), pltpu.VMEM((rows, LANES), F32),
                        pltpu.VMEM((rows, KV_LORA), F32)],
        compiler_params=_params(("parallel", "arbitrary")),
        name="attn_prompt",
    )(Q, K, KT)


def _attn_decode_kernel(pt_ref, q_ref, kn_ref, ckv_hbm, krt_hbm, o_ref, ckv_buf, krt_buf, sem, m_scr, l_scr, acc_scr,
                        *, layer, n_pages, page, group, L, scale):
    b = pl.program_id(0)
    nb = pl.num_programs(0)
    slot = b % 2
    rows = q_ref.shape[1]

    def page_copies(seq, sl, p):
        pid = pt_ref[seq, p]
        return (pltpu.make_async_copy(ckv_hbm.at[layer, pid], ckv_buf.at[sl, pl.ds(p * page, page)], sem.at[sl, 0]),
                pltpu.make_async_copy(krt_hbm.at[layer, pid], krt_buf.at[sl, :, pl.ds(p * page, page)], sem.at[sl, 1]))

    def start_gather(seq, sl):
        def body(p, carry):
            for cp in page_copies(seq, sl, p):
                cp.start()
            return carry
        lax.fori_loop(0, n_pages, body, 0)

    def wait_gather(seq, sl):
        def body(p, carry):
            for cp in page_copies(seq, sl, p):
                cp.wait()
            return carry
        lax.fori_loop(0, n_pages, body, 0)

    @pl.when(b == 0)
    def _():
        start_gather(0, 0)

    @pl.when(b + 1 < nb)
    def _():
        start_gather(b + 1, 1 - slot)

    m_scr[...] = jnp.full((rows, 1), NEG_BIG, F32)
    l_scr[...] = jnp.zeros((rows, 1), F32)
    acc_scr[...] = jnp.zeros((rows, KV_LORA), F32)
    q = q_ref[0]
    ql = q[:, 0:KV_LORA]
    qr = q[:, KV_LORA:KV_LORA + MLA_ROPE]

    def update(s, vals):
        m_old = m_scr[...]
        m_new = jnp.maximum(m_old, jnp.max(s, axis=1, keepdims=True))
        alpha = jnp.exp(m_old - m_new)
        p = jnp.exp(s - m_new)
        l_scr[...] = alpha * l_scr[...] + jnp.sum(p, axis=1, keepdims=True)
        acc_scr[...] = alpha * acc_scr[...] + _dot(p.astype(BF16), vals)
        m_scr[...] = m_new

    wait_gather(b, slot)
    keys = group * page

    def body(g, carry):
        off = pl.multiple_of(g * keys, keys)
        kc = ckv_buf[slot, pl.ds(off, keys), :].astype(BF16)
        krt = krt_buf[slot, :, pl.ds(off, keys)].astype(BF16)
        update((_dot_nt(ql, kc) + _dot(qr, krt)) * scale, kc)
        return carry

    lax.fori_loop(0, n_pages // group, body, 0)
    kn = kn_ref[0]
    s = _dot_nt(q, kn)
    tpos = lax.broadcasted_iota(jnp.int32, (rows, L), 0) % L
    jpos = lax.broadcasted_iota(jnp.int32, (rows, L), 1)
    update(jnp.where(jpos <= tpos, s * scale, NEG_BIG), kn[:, 0:KV_LORA])
    o_ref[0] = acc_scr[...] / l_scr[...]


def attn_decode(Qs, Kn, cache_ckv, cache_krt, page_table, layer, scale):
    Bd, rows, _ = Qs.shape
    L = Kn.shape[1]
    n_pages = page_table.shape[1]
    page = cache_ckv.shape[2]
    group = min(PAGES_PER_STEP, n_pages)
    assert n_pages % group == 0
    kern = functools.partial(_attn_decode_kernel, layer=layer, n_pages=n_pages, page=page, group=group, L=L,
                             scale=scale)
    grid_spec = pltpu.PrefetchScalarGridSpec(
        num_scalar_prefetch=1,
        grid=(Bd,),
        in_specs=[pl.BlockSpec((1, rows, QK_WIDTH), lambda b, pt: (b, 0, 0)),
                  pl.BlockSpec((1, L, QK_WIDTH), lambda b, pt: (b, 0, 0)),
                  pl.BlockSpec(memory_space=pl.ANY), pl.BlockSpec(memory_space=pl.ANY)],
        out_specs=pl.BlockSpec((1, rows, KV_LORA), lambda b, pt: (b, 0, 0)),
        scratch_shapes=[pltpu.VMEM((2, n_pages * page, KV_LORA), F32),
                        pltpu.VMEM((2, MLA_ROPE, n_pages * page), F32),
                        pltpu.SemaphoreType.DMA((2, 2)),
                        pltpu.VMEM((rows, 1), F32), pltpu.VMEM((rows, 1), F32), pltpu.VMEM((rows, KV_LORA), F32)],
    )
    return pl.pallas_call(
        kern,
        out_shape=jax.ShapeDtypeStruct((Bd, rows, KV_LORA), F32),
        grid_spec=grid_spec,
        compiler_params=_params(("arbitrary",)),
        name="attn_decode",
    )(page_table, Qs, Kn, cache_ckv, cache_krt)


def _rope_tables(pos):
    half = MLA_ROPE // 2
    inv = ROPE_THETA ** (-jnp.arange(half, dtype=F32) / half)
    ang = pos.astype(F32)[:, None] * inv[None, :]
    cos, sin = jnp.cos(ang), jnp.sin(ang)
    z = jnp.zeros((pos.shape[0], LANES - MLA_ROPE), F32)
    return jnp.concatenate([cos, cos, z], axis=1), jnp.concatenate([-sin, sin, z], axis=1)


def _expand_blockdiag(w):
    G = w.shape[0]
    per = BD_CHUNK // QKV_BLOCK
    w4 = w.reshape(G // per, per, QKV_BLOCK, QKV_BLOCK)
    eye = jnp.eye(per, dtype=w.dtype)
    dense = jnp.einsum('cgio,gh->cgiho', w4, eye)
    return dense.reshape(G // per, BD_CHUNK, BD_CHUNK).astype(BF16)


def _prep_even(e, ev_norm_w, ev_w_in, hgrn_norm_w, mla_q_norm_w, mla_w_q_b, mla_kv_norm_w, mla_w_uk, mla_w_uv, ev_w_out):
    W = HGRN_HEADS * HGRN_DK
    w = ev_w_in[e]
    o = 0
    parts = {}
    for name, width in (("q_a", W), ("f_a", W), ("i_a", W), ("g_a", W), ("cq", Q_LORA), ("ckv", KV_LORA),
                        ("kpe", MLA_ROPE), ("g_b", MLA_HEADS * MLA_V)):
        parts[name] = w[:, o:o + width]
        o += width
    zpad = jnp.zeros((w.shape[0], LANES - MLA_ROPE), w.dtype)
    w_in = jnp.concatenate([parts[n] for n in ("q_a", "f_a", "i_a", "g_a", "g_b", "cq", "ckv", "kpe")] + [zpad], axis=1)
    wqb = mla_w_q_b[e].reshape(Q_LORA, MLA_HEADS, MLA_NOPE + MLA_ROPE)
    nope = wqb[:, :, :MLA_NOPE].reshape(Q_LORA, MLA_HEADS * MLA_NOPE)
    pe = jnp.pad(wqb[:, :, MLA_NOPE:], ((0, 0), (0, 0), (0, LANES - MLA_ROPE))).reshape(Q_LORA, MLA_HEADS * LANES)
    return dict(
        nw=ev_norm_w[e][None, :], w_in=w_in.astype(BF16),
        hnw=hgrn_norm_w[e][None, :], qnw=mla_q_norm_w[e][None, :], kvnw=mla_kv_norm_w[e][None, :],
        wqb=jnp.concatenate([nope, pe], axis=1).astype(BF16),
        wuk_t=jnp.transpose(mla_w_uk[e], (1, 2, 0)).astype(BF16),
        wuv=jnp.transpose(mla_w_uv[e], (1, 0, 2)).astype(BF16),
        wout=ev_w_out[e].astype(BF16),
    )


def _prep_odd(e, od_norm_w, od_w_in, conv_w, conv_b, w_q, w_k, w_v, w_gate, b_gate, norm_w, skip, od_w_out):
    ng = w_gate.shape[2]
    return dict(
        nw=od_norm_w[e][None, :], w_in=od_w_in[e].astype(BF16),
        cw=conv_w[e], cb=conv_b[e][None, :],
        wq=_expand_blockdiag(w_q[e]), wk=_expand_blockdiag(w_k[e]), wv=_expand_blockdiag(w_v[e]),
        wg=jnp.pad(w_gate[e], ((0, 0), (0, LANES - ng))).astype(BF16),
        bg=jnp.pad(b_gate[e], (0, LANES - ng))[None, :],
        lnw=norm_w[e][None, :], skip=skip[e][None, :], wout=od_w_out[e].astype(BF16),
    )


def _even_layer(x, p, lb, cos, sin, n_seq, seq_len, Lc, state, prev_stack, layer, n_layers, attend):
    W = HGRN_HEADS * HGRN_DK
    u = norm_matmul(x, p["nw"], p["w_in"])
    Q, K, KT, c, kr = mla_prep(u, cos, sin, p["qnw"], p["wqb"], p["wuk_t"], p["kvnw"], col0=5 * W)
    o_a, S_stack = hgrn_scan(u.reshape(n_seq, seq_len, u.shape[1]), lb, Lc, state, prev_stack, layer, n_layers)
    o_b = attend(Q, K, KT)
    x = even_out(x, o_a.reshape(x.shape[0], W), u, o_b, p["hnw"], p["wuv"], p["wout"], ga_blk=3, gb_blk=4)
    return x, c, kr, S_stack


def _odd_layer(x, p, n_seq, seq_len, Lc, pad, state, prev_stack, layer, n_layers, conv_buf):
    u = norm_matmul(x, p["nw"], p["w_in"])
    D = u.shape[1] // 2
    dh = D // MLSTM_HEADS
    u3 = u.reshape(n_seq, seq_len, 2 * D)
    if conv_buf is None:
        shifts = None
    else:
        xp = jnp.concatenate([conv_buf, u3[:, :, :D]], axis=1)
        shifts = [xp[:, CONV_W - 1 - s: CONV_W - 1 - s + seq_len].reshape(n_seq * seq_len, D) for s in (1, 2, 3)]
    xc, q, ks, v, g = odd_mid(u, shifts, seq_len, p["cw"], p["cb"], p["wq"], p["wk"], p["wv"], p["wg"], p["bg"],
                              k_scale=dh ** -0.5)
    r3 = lambda a: a.reshape(n_seq, seq_len, a.shape[1])
    h, C_stack, n, m = mlstm_scan(r3(q), r3(ks), r3(v), r3(g), Lc, pad, state, prev_stack, layer, n_layers)
    x = odd_out(x, h.reshape(x.shape[0], D), xc, u, p["lnw"], p["skip"], p["wout"])
    new_buf = u3[:, seq_len - (CONV_W - 1):, :D]
    return x, C_stack, n[:, :, 0, :], m[:, :, 0, 0], new_buf


def kernel(x_prompt, x_sample, cache_mla_ckv, cache_mla_krope, state_hgrn, state_mlstm_C, state_mlstm_n, state_mlstm_m, state_mlstm_conv, page_table, meta_tokens, ev_norm_w, ev_w_in, hgrn_lower_bounds, hgrn_norm_w, mla_q_norm_w, mla_w_q_b, mla_kv_norm_w, mla_w_uk, mla_w_uv, ev_w_out, od_norm_w, od_w_in, mlstm_conv_w, mlstm_conv_b, mlstm_w_q, mlstm_w_k, mlstm_w_v, mlstm_w_gate, mlstm_b_gate, mlstm_norm_w, mlstm_skip, od_w_out, final_norm_w):
    B, seq, Dm = x_prompt.shape
    Bd, Ls, _ = x_sample.shape
    NE, NO = ev_norm_w.shape[0], od_norm_w.shape[0]
    depth = NE + NO
    T = seq + N_META
    Tp = -(-T // ROW_ALIGN) * ROW_ALIGN
    pad = Tp - T
    assert seq % ROW_BLOCK == 0 and (pad + N_META) % ROW_BLOCK == 0 and (Bd * Ls) % ROW_BLOCK == 0
    assert Ls >= CONV_W - 1
    past_len = page_table.shape[1] * cache_mla_ckv.shape[2]
    scale = (MLA_NOPE + MLA_ROPE) ** -0.5
    cache_krt = jnp.swapaxes(cache_mla_krope, 2, 3)
    mlstm_state = (state_mlstm_C, state_mlstm_n[:, :, :, None, :],
                   jnp.broadcast_to(state_mlstm_m[:, :, :, None, None], state_mlstm_m.shape + (1, LANES)))

    hp = jnp.concatenate([jnp.zeros((B, pad, Dm), F32), jnp.broadcast_to(meta_tokens[None], (B, N_META, Dm)), x_prompt],
                         axis=1).reshape(B * Tp, Dm)
    hs = x_sample.reshape(Bd * Ls, Dm)
    cos_p, sin_p = _rope_tables(jnp.arange(Tp) - pad)
    cos_p, sin_p = jnp.tile(cos_p, (B, 1)), jnp.tile(sin_p, (B, 1))
    cos_s, sin_s = _rope_tables(past_len + jnp.arange(Ls))
    cos_s, sin_s = jnp.tile(cos_s, (Bd, 1)), jnp.tile(sin_s, (Bd, 1))
    sm = jax.nn.softmax(hgrn_lower_bounds.astype(F32), axis=0)
    lbs = jnp.cumsum(sm, axis=0) - sm[0]

    def attend_prompt(Q, K, KT):
        return attn_prompt(Q, K, KT, B, Tp, pad, scale)

    def make_attend_sample(e):
        def attend(Q, K, KT):
            Qs = Q.reshape(MLA_HEADS, Bd, Ls, QK_WIDTH).transpose(1, 0, 2, 3).reshape(Bd, MLA_HEADS * Ls, QK_WIDTH)
            o = attn_decode(Qs, K.reshape(Bd, Ls, QK_WIDTH), cache_mla_ckv, cache_krt, page_table, e, scale)
            return o.reshape(Bd, MLA_HEADS, Ls, KV_LORA).transpose(1, 0, 2, 3).reshape(MLA_HEADS, Bd * Ls, KV_LORA)
        return attend

    outs_p = {k: [] for k in ("ckv", "kr", "n", "m", "cv")}
    outs_s = {k: [] for k in ("ckv", "kr", "n", "m", "cv")}
    S_p = S_s = C_p = C_s = None
    for l in range(depth):
        e = l // 2
        if l % 2 == 0:
            p = _prep_even(e, ev_norm_w, ev_w_in, hgrn_norm_w, mla_q_norm_w, mla_w_q_b, mla_kv_norm_w, mla_w_uk,
                           mla_w_uv, ev_w_out)
            lb = lbs[e][None, :]
            hp, c, kr, S_p = _even_layer(hp, p, lb, cos_p, sin_p, B, Tp, HGRN_CHUNK, None, S_p, e, NE, attend_prompt)
            outs_p["ckv"].append(c.reshape(B, Tp, KV_LORA)[:, pad:])
            outs_p["kr"].append(kr.reshape(B, Tp, MLA_ROPE)[:, pad:])
            hs, c, kr, S_s = _even_layer(hs, p, lb, cos_s, sin_s, Bd, Ls, Ls, state_hgrn, S_s, e, NE,
                                         make_attend_sample(e))
            outs_s["ckv"].append(c.reshape(Bd, Ls, KV_LORA))
            outs_s["kr"].append(kr.reshape(Bd, Ls, MLA_ROPE))
        else:
            p = _prep_odd(e, od_norm_w, od_w_in, mlstm_conv_w, mlstm_conv_b, mlstm_w_q, mlstm_w_k, mlstm_w_v,
                          mlstm_w_gate, mlstm_b_gate, mlstm_norm_w, mlstm_skip, od_w_out)
            hp, C_p, n, m, cv = _odd_layer(hp, p, B, Tp, MLSTM_CHUNK, pad, None, C_p, e, NO, None)
            for key, val in zip(("n", "m", "cv"), (n, m, cv)):
                outs_p[key].append(val)
            hs, C_s, n, m, cv = _odd_layer(hs, p, Bd, Ls, Ls, 0, mlstm_state, C_s, e, NO, state_mlstm_conv[e])
            for key, val in zip(("n", "m", "cv"), (n, m, cv)):
                outs_s[key].append(val)

    fw = final_norm_w[None, :]
    blocks_in = Tp // ROW_BLOCK
    skip_blocks = (pad + N_META) // ROW_BLOCK
    y_prompt = final_norm(hp, fw, skip_blocks, blocks_in, blocks_in - skip_blocks).reshape(B, seq, Dm)
    y_sample = final_norm(hs, fw, 0, (Bd * Ls) // ROW_BLOCK, (Bd * Ls) // ROW_BLOCK).reshape(Bd, Ls, Dm)
    st = lambda xs: jnp.stack(xs)
    return (y_prompt, y_sample,
            st(outs_p["ckv"]), st(outs_p["kr"]), S_p, C_p, st(outs_p["n"]), st(outs_p["m"]), st(outs_p["cv"]),
            st(outs_s["ckv"]), st(outs_s["kr"]), S_s, C_s, st(outs_s["n"]), st(outs_s["m"]), st(outs_s["cv"]))
```

```python
import functools

import jax
import jax.numpy as jnp
from jax import lax
from jax.experimental import pallas as pl
from jax.experimental.pallas import tpu as pltpu

F32 = jnp.float32
BF16 = jnp.bfloat16

N_META = 16
HGRN_HEADS = 8
HGRN_DK = 128
MLA_HEADS = 8
MLA_NOPE = 128
MLA_ROPE = 64
MLA_V = 128
Q_LORA = 512
KV_LORA = 256
ROPE_THETA = 10000.0
MLSTM_HEADS = 4
QKV_BLOCK = 4
CONV_W = 4
NEG_BIG = -1e30
RMS_EPS = 1e-6
LN_EPS = 1e-5
LOG2E = 1.4426950408889634

LANES = 128
ROW_ALIGN = 256
ROW_BLOCK = 256
QK_WIDTH = KV_LORA + LANES
Q_BLOCK = 128
KV_BLOCK = 512
ATTN_ROW_SPLITS = 1
HGRN_CHUNK = 128
MLSTM_CHUNK = 256
SUB = 16
PAGES_PER_STEP = 16
HGRN_SAFE_EXP = 60.0
BD_CHUNK = 256
VMEM_LIMIT = 56 * 1024 * 1024

_NT = (((1,), (1,)), ((), ()))
_TN = (((0,), (0,)), ((), ()))


def _dot(a, b):
    return jnp.dot(a, b, preferred_element_type=F32)


def _dot_nt(a, b):
    return lax.dot_general(a, b, _NT, preferred_element_type=F32)


def _dot_tn(a, b):
    return lax.dot_general(a, b, _TN, preferred_element_type=F32)


def _params(sem):
    return pltpu.CompilerParams(dimension_semantics=sem, vmem_limit_bytes=VMEM_LIMIT)


def _rms(x, w, eps=RMS_EPS):
    return x * lax.rsqrt(jnp.mean(x * x, axis=-1, keepdims=True) + eps) * w


def _silu(x):
    return x * jax.nn.sigmoid(x)


def _const_spec(shape):
    nd = len(shape)
    return pl.BlockSpec(shape, lambda *_: (0,) * nd, pipeline_mode=pl.Buffered(1))


def _norm_matmul_kernel(x_ref, nw_ref, w_ref, o_ref, *, chunks):
    xn = _rms(x_ref[...], nw_ref[...]).astype(BF16)
    for start, width in chunks:
        o_ref[:, start:start + width] = _dot(xn, w_ref[:, start:start + width])


def norm_matmul(x, nw, w):
    R, D = x.shape
    N = w.shape[1]
    tm = ROW_BLOCK
    chunks, s = [], 0
    while s < N:
        width = min(1024, N - s)
        chunks.append((s, width))
        s += width
    return pl.pallas_call(
        functools.partial(_norm_matmul_kernel, chunks=tuple(chunks)),
        out_shape=jax.ShapeDtypeStruct((R, N), F32),
        grid=(R // tm,),
        in_specs=[pl.BlockSpec((tm, D), lambda i: (i, 0)), _const_spec((1, D)), _const_spec((D, N))],
        out_specs=pl.BlockSpec((tm, N), lambda i: (i, 0)),
        compiler_params=_params(("parallel",)),
        name="norm_matmul",
    )(x, nw, w)


def _rope_tile(x, cos, sin):
    lane = lax.broadcasted_iota(jnp.int32, x.shape, 1)
    half = MLA_ROPE // 2
    partner = jnp.where(lane % MLA_ROPE < half, pltpu.roll(x, LANES - half, 1), pltpu.roll(x, half, 1))
    return x * cos + partner * sin


def _mla_prep_kernel(cq_ref, ckv_ref, kpe_ref, cos_ref, sin_ref, kbias_ref, qnw_ref, wqb_ref, wuk_ref, kvnw_ref,
                     q_ref, k_ref, kt_ref, c_ref, kr_ref):
    cos = cos_ref[...]
    sin = sin_ref[...]
    lane = lax.broadcasted_iota(jnp.int32, (1, LANES), 1)
    q_one = jnp.where(lane == MLA_ROPE, 1.0, 0.0)
    cqn = _rms(cq_ref[...], qnw_ref[...]).astype(BF16)
    qb = _dot(cqn, wqb_ref[...])
    nope_w = MLA_HEADS * MLA_NOPE
    for h in range(MLA_HEADS):
        qn = qb[:, h * MLA_NOPE:(h + 1) * MLA_NOPE].astype(BF16)
        q_ref[h, :, 0:KV_LORA] = _dot(qn, wuk_ref[h]).astype(BF16)
        qpe = qb[:, nope_w + h * LANES: nope_w + (h + 1) * LANES]
        q_ref[h, :, KV_LORA:QK_WIDTH] = (_rope_tile(qpe, cos, sin) + q_one).astype(BF16)
    c = _rms(ckv_ref[...], kvnw_ref[...])
    kr = _rope_tile(kpe_ref[...], cos, sin)
    c_ref[...] = c
    kr_ref[...] = kr[:, 0:MLA_ROPE]
    krb = kr + kbias_ref[...]
    k_ref[:, 0:KV_LORA] = c.astype(BF16)
    k_ref[:, KV_LORA:QK_WIDTH] = krb.astype(BF16)
    kt_ref[0:KV_LORA, :] = c.T.astype(BF16)
    kt_ref[KV_LORA:QK_WIDTH, :] = krb.T.astype(BF16)


def mla_prep(u, cos, sin, kbias, qnw, wqb, wuk_t, kvnw, col0):
    R = u.shape[0]
    tm = ROW_BLOCK
    cq_blk = col0 // Q_LORA
    ckv_blk = (col0 + Q_LORA) // KV_LORA
    kpe_blk = (col0 + Q_LORA + KV_LORA) // LANES
    return pl.pallas_call(
        _mla_prep_kernel,
        out_shape=(jax.ShapeDtypeStruct((MLA_HEADS, R, QK_WIDTH), BF16),
                   jax.ShapeDtypeStruct((R, QK_WIDTH), BF16),
                   jax.ShapeDtypeStruct((QK_WIDTH, R), BF16),
                   jax.ShapeDtypeStruct((R, KV_LORA), F32),
                   jax.ShapeDtypeStruct((R, MLA_ROPE), F32)),
        grid=(R // tm,),
        in_specs=[pl.BlockSpec((tm, Q_LORA), lambda i: (i, cq_blk)),
                  pl.BlockSpec((tm, KV_LORA), lambda i: (i, ckv_blk)),
                  pl.BlockSpec((tm, LANES), lambda i: (i, kpe_blk)),
                  pl.BlockSpec((tm, LANES), lambda i: (i, 0)),
                  pl.BlockSpec((tm, LANES), lambda i: (i, 0)),
                  pl.BlockSpec((tm, LANES), lambda i: (i, 0)),
                  _const_spec(qnw.shape), _const_spec(wqb.shape), _const_spec(wuk_t.shape), _const_spec(kvnw.shape)],
        out_specs=(pl.BlockSpec((MLA_HEADS, tm, QK_WIDTH), lambda i: (0, i, 0)),
                   pl.BlockSpec((tm, QK_WIDTH), lambda i: (i, 0)),
                   pl.BlockSpec((QK_WIDTH, tm), lambda i: (0, i)),
                   pl.BlockSpec((tm, KV_LORA), lambda i: (i, 0)),
                   pl.BlockSpec((tm, MLA_ROPE), lambda i: (i, 0))),
        compiler_params=_params(("parallel",)),
        name="mla_prep",
    )(u, u, u, cos, sin, kbias, qnw, wqb, wuk_t, kvnw)


def _even_out_kernel(x_ref, oa_ref, ga_ref, gb_ref, ob_ref, hnw_ref, wuv_ref, wout_ref, o_ref, y_scr):
    hw = HGRN_HEADS * HGRN_DK
    for h in range(HGRN_HEADS):
        sl = slice(h * HGRN_DK, (h + 1) * HGRN_DK)
        ya = _rms(oa_ref[:, sl], hnw_ref[:, sl])
        y_scr[:, sl] = (ya * _silu(ga_ref[:, sl])).astype(BF16)
    for h in range(MLA_HEADS):
        sl = slice(h * MLA_V, (h + 1) * MLA_V)
        yb = _dot(ob_ref[h].astype(BF16), wuv_ref[h])
        y_scr[:, hw + h * MLA_V: hw + (h + 1) * MLA_V] = (yb * _silu(gb_ref[:, sl])).astype(BF16)
    o_ref[...] = x_ref[...] + _dot(y_scr[...], wout_ref[...])


def even_out(x, o_a, u, o_b, hnw, wuv, wout, ga_blk, gb_blk):
    R, D = x.shape
    tm = ROW_BLOCK
    W = HGRN_HEADS * HGRN_DK
    return pl.pallas_call(
        _even_out_kernel,
        out_shape=jax.ShapeDtypeStruct((R, D), F32),
        grid=(R // tm,),
        in_specs=[pl.BlockSpec((tm, D), lambda i: (i, 0)),
                  pl.BlockSpec((tm, W), lambda i: (i, 0)),
                  pl.BlockSpec((tm, W), lambda i: (i, ga_blk)),
                  pl.BlockSpec((tm, W), lambda i: (i, gb_blk)),
                  pl.BlockSpec((MLA_HEADS, tm, KV_LORA), lambda i: (0, i, 0)),
                  _const_spec(hnw.shape), _const_spec(wuv.shape), _const_spec(wout.shape)],
        out_specs=pl.BlockSpec((tm, D), lambda i: (i, 0)),
        scratch_shapes=[pltpu.VMEM((tm, 2 * W), BF16)],
        compiler_params=_params(("parallel",)),
        name="even_out",
    )(x, o_a, u, u, o_b, hnw, wuv, wout)


def _odd_mid_kernel(*refs, tm, blocks_per_seq, shifted_inputs, k_scale):
    if shifted_inputs:
        x_ref, x1_ref, x2_ref, x3_ref = refs[:4]
        rest = refs[4:]
    else:
        x_ref, halo_ref = refs[:2]
        rest = refs[2:]
    cw_ref, cb_ref, wq_ref, wk_ref, wv_ref, wg_ref, bg_ref, xc_ref, q_ref, k_ref, v_ref, g_ref = rest[:12]
    x = x_ref[...]
    if shifted_inputs:
        shifted = [x1_ref[...], x2_ref[...], x3_ref[...]]
    else:
        xs = rest[12]
        first = (pl.program_id(0) % blocks_per_seq) == 0
        halo = halo_ref[...]
        xs[0:8, :] = jnp.where(first, jnp.zeros_like(halo), halo)
        xs[8:, :] = x
        shifted = [xs[pl.ds(8 - s, tm), :] for s in (1, 2, 3)]
    cw = cw_ref[...]
    pre = cb_ref[...] + x * cw[3:4] + shifted[0] * cw[2:3] + shifted[1] * cw[1:2] + shifted[2] * cw[0:1]
    xc = _silu(pre)
    xc_ref[...] = xc
    D = x.shape[1]
    g = jnp.zeros((x.shape[0], LANES), F32) + bg_ref[...]
    for c in range(D // BD_CHUNK):
        sl = slice(c * BD_CHUNK, (c + 1) * BD_CHUNK)
        xcb = xc[:, sl].astype(BF16)
        q = _dot(xcb, wq_ref[c])
        k = _dot(xcb, wk_ref[c])
        v = _dot(x[:, sl].astype(BF16), wv_ref[c])
        q_ref[:, sl] = q.astype(BF16)
        k_ref[:, sl] = (k * k_scale).astype(BF16)
        v_ref[:, sl] = v.astype(BF16)
        g = g + _dot(q.astype(BF16), wg_ref[c * BD_CHUNK:(c + 1) * BD_CHUNK, :])
        g = g + _dot(k.astype(BF16), wg_ref[D + c * BD_CHUNK: D + (c + 1) * BD_CHUNK, :])
        g = g + _dot(v.astype(BF16), wg_ref[2 * D + c * BD_CHUNK: 2 * D + (c + 1) * BD_CHUNK, :])
    g_ref[...] = g


def odd_mid(u, shifts, rows_per_seq, cw, cb, wq, wk, wv, wg, bg, k_scale):
    R = u.shape[0]
    D = u.shape[1] // 2
    tm = ROW_BLOCK
    row = lambda i: (i, 0)
    wspecs = [_const_spec(a.shape) for a in (cw, cb, wq, wk, wv, wg, bg)]
    outs = ((jax.ShapeDtypeStruct((R, D), F32),) + tuple(jax.ShapeDtypeStruct((R, D), BF16) for _ in range(3))
            + (jax.ShapeDtypeStruct((R, LANES), F32),))
    out_specs = tuple(pl.BlockSpec((tm, D), row) for _ in range(4)) + (pl.BlockSpec((tm, LANES), row),)
    if shifts is None:
        ins = [pl.BlockSpec((tm, D), row),
               pl.BlockSpec((8, D), lambda i: (jnp.maximum(i * (tm // 8) - 1, 0), 0))]
        args = (u, u)
        scratch = [pltpu.VMEM((tm + 8, D), F32)]
    else:
        ins = [pl.BlockSpec((tm, D), row)] * 4
        args = (u,) + tuple(shifts)
        scratch = []
    kern = functools.partial(_odd_mid_kernel, tm=tm, blocks_per_seq=rows_per_seq // tm,
                             shifted_inputs=shifts is not None, k_scale=k_scale)
    return pl.pallas_call(
        kern, out_shape=outs, grid=(R // tm,), in_specs=ins + wspecs, out_specs=out_specs,
        scratch_shapes=scratch, compiler_params=_params(("parallel",)), name="odd_mid",
    )(*args, cw, cb, wq, wk, wv, wg, bg)


def _odd_out_kernel(x_ref, h_ref, xc_ref, z_ref, nw_ref, skip_ref, wout_ref, o_ref, y_scr, *, dh):
    D = h_ref.shape[1]
    for hd in range(D // dh):
        sl = slice(hd * dh, (hd + 1) * dh)
        h = h_ref[:, sl]
        mu = jnp.mean(h, axis=-1, keepdims=True)
        hc = h - mu
        var = jnp.mean(hc * hc, axis=-1, keepdims=True)
        hn = hc * lax.rsqrt(var + LN_EPS) * nw_ref[:, sl]
        y = (hn + skip_ref[:, sl] * xc_ref[:, sl]) * _silu(z_ref[:, sl])
        y_scr[:, sl] = y.astype(BF16)
    o_ref[...] = x_ref[...] + _dot(y_scr[...], wout_ref[...])


def odd_out(x, h, xc, u, nw, skip, wout):
    R, Dm = x.shape
    D = h.shape[1]
    tm = ROW_BLOCK
    row = lambda i: (i, 0)
    return pl.pallas_call(
        functools.partial(_odd_out_kernel, dh=D // MLSTM_HEADS),
        out_shape=jax.ShapeDtypeStruct((R, Dm), F32),
        grid=(R // tm,),
        in_specs=[pl.BlockSpec((tm, Dm), row), pl.BlockSpec((tm, D), row), pl.BlockSpec((tm, D), row),
                  pl.BlockSpec((tm, D), lambda i: (i, 1)),
                  _const_spec(nw.shape), _const_spec(skip.shape), _const_spec(wout.shape)],
        out_specs=pl.BlockSpec((tm, Dm), row),
        scratch_shapes=[pltpu.VMEM((tm, D), BF16)],
        compiler_params=_params(("parallel",)),
        name="odd_out",
    )(x, h, xc, u, nw, skip, wout)


def _final_norm_kernel(x_ref, w_ref, o_ref):
    o_ref[...] = _rms(x_ref[...], w_ref[...])


def final_norm(x, w, skip_blocks, blocks_per_seq_in, blocks_per_seq_out):
    R, D = x.shape
    tm = ROW_BLOCK
    n_seq = R // (blocks_per_seq_in * tm)
    Ro = n_seq * blocks_per_seq_out * tm

    def in_map(i):
        return ((i // blocks_per_seq_out) * blocks_per_seq_in + skip_blocks + i % blocks_per_seq_out, 0)

    return pl.pallas_call(
        _final_norm_kernel,
        out_shape=jax.ShapeDtypeStruct((Ro, D), F32),
        grid=(Ro // tm,),
        in_specs=[pl.BlockSpec((tm, D), in_map), _const_spec(w.shape)],
        out_specs=pl.BlockSpec((tm, D), lambda i: (i, 0)),
        compiler_params=_params(("parallel",)),
        name="final_norm",
    )(x, w)


def _split3(x):
    hi = x.astype(BF16)
    r1 = x - hi.astype(F32)
    mid = r1.astype(BF16)
    lo = (r1 - mid.astype(F32)).astype(BF16)
    return hi, mid, lo


def _cumsum_rows(x):
    L = x.shape[0]
    r = lax.broadcasted_iota(jnp.int32, (L, L), 0)
    c = lax.broadcasted_iota(jnp.int32, (L, L), 1)
    tri = (c <= r).astype(BF16)
    hi, mid, lo = _split3(x)
    return _dot(tri, hi) + _dot(tri, mid) + _dot(tri, lo)


def _row_to_col(row):
    n = row.shape[1]
    r = lax.broadcasted_iota(jnp.int32, (n, n), 0)
    c = lax.broadcasted_iota(jnp.int32, (n, n), 1)
    return jnp.sum(jnp.where(r == c, row, 0.0), axis=1, keepdims=True)


def _hgrn_chunk_kernel(*refs, Lc, sub, has_state, has_alias):
    qa_ref, fa_ref, ia_ref, lb_ref = refs[:4]
    pos = 4
    s0_ref = None
    if has_state:
        s0_ref = refs[pos]
        pos += 1
    if has_alias:
        pos += 1
    o_ref, sout_ref, s_scr = refs[pos:pos + 3]
    H, K = HGRN_HEADS, HGRN_DK

    @pl.when(pl.program_id(1) == 0)
    def _():
        s_scr[...] = s0_ref[0] if has_state else jnp.zeros_like(s_scr)

    lb = lb_ref[...]
    qa = qa_ref[0]
    fa = fa_ref[0]
    q = _silu(qa)
    logf = jnp.log(lb + (1.0 - lb) * jax.nn.sigmoid(fa))
    k = (1.0 - lb) * jax.nn.sigmoid(-fa)
    vb = ia_ref[0].astype(BF16)
    G = _cumsum_rows(logf)
    qg = q * jnp.exp(G)
    nblk = Lc // sub
    worst = -G[sub - 1:sub]
    for i in range(1, nblk):
        worst = jnp.maximum(worst, G[i * sub - 1:i * sub] - G[(i + 1) * sub - 1:(i + 1) * sub])
    safe = jnp.max(worst) <= HGRN_SAFE_EXP
    row = lax.broadcasted_iota(jnp.int32, (sub, Lc), 0)
    col = lax.broadcasted_iota(jnp.int32, (sub, Lc), 1)
    tr = lax.broadcasted_iota(jnp.int32, (sub, sub), 0)
    tc = lax.broadcasted_iota(jnp.int32, (sub, sub), 1)

    def inter(h):
        sl = slice(h * K, (h + 1) * K)
        return sl, _dot(qg[:, sl].astype(BF16), s_scr[h].astype(BF16))

    @pl.when(safe)
    def _():
        for h in range(H):
            sl, o = inter(h)
            Gh, qh, kh = G[:, sl], q[:, sl], k[:, sl]
            a_rows = []
            for i in range(nblk):
                lo = i * sub
                r = Gh[lo - 1:lo] if i > 0 else jnp.zeros((1, K), F32)
                qd = (qh[lo:lo + sub] * jnp.exp(Gh[lo:lo + sub] - r)).astype(BF16)
                kd = (kh * jnp.exp(jnp.minimum(r - Gh, HGRN_SAFE_EXP))).astype(BF16)
                a_rows.append(jnp.where(col <= lo + row, _dot_nt(qd, kd), 0.0))
            a = a_rows[0] if nblk == 1 else jnp.concatenate(a_rows, axis=0)
            o_ref[0, :, sl] = o + _dot(a.astype(BF16), vb[:, sl])

    @pl.when(jnp.logical_not(safe))
    def _():
        for h in range(H):
            sl, o_int = inter(h)
            Gh, qh, kh, vh = G[:, sl], q[:, sl], k[:, sl], vb[:, sl]
            for i in range(nblk):
                lo = i * sub
                Gi, qi, ki = Gh[lo:lo + sub], qh[lo:lo + sub], kh[lo:lo + sub]
                e = jnp.exp(jnp.minimum(Gi[:, None, :] - Gi[None, :, :], 0.0))
                ad = jnp.sum(qi[:, None, :] * ki[None, :, :] * e, axis=-1)
                ad = jnp.where(tc <= tr, ad, 0.0)
                o = o_int[lo:lo + sub] + _dot(ad.astype(BF16), vh[lo:lo + sub])
                if i > 0:
                    r = Gh[lo - 1:lo]
                    qd = (qi * jnp.exp(Gi - r)).astype(BF16)
                    kd = (kh[:lo] * jnp.exp(r - Gh[:lo])).astype(BF16)
                    o = o + _dot(_dot_nt(qd, kd).astype(BF16), vh[:lo])
                o_ref[0, lo:lo + sub, sl] = o

    GL = G[Lc - 1:Lc]
    kd_all = (k * jnp.exp(GL - G)).astype(BF16)
    eGL = jnp.exp(GL)
    for h in range(H):
        sl = slice(h * K, (h + 1) * K)
        S_new = _row_to_col(eGL[:, sl]) * s_scr[h] + _dot_tn(kd_all[:, sl], vb[:, sl])
        s_scr[h] = S_new
        sout_ref[0, h] = S_new


def hgrn_scan(u3, lb, Lc, state, prev_stack, layer, n_layers):
    B, T, _ = u3.shape
    H, K = HGRN_HEADS, HGRN_DK
    W = H * K
    nc = T // Lc
    sub = min(SUB, Lc)
    st_spec = pl.BlockSpec((None, 1, H, K, K), lambda b, c: (layer, b, 0, 0, 0))
    in_specs = [pl.BlockSpec((1, Lc, W), lambda b, c: (b, c, 0)),
                pl.BlockSpec((1, Lc, W), lambda b, c: (b, c, 1)),
                pl.BlockSpec((1, Lc, W), lambda b, c: (b, c, 2)),
                pl.BlockSpec((1, W), lambda b, c: (0, 0))]
    args = [u3, u3, u3, lb]
    if state is not None:
        in_specs.append(st_spec)
        args.append(state)
    aliases = {}
    if prev_stack is not None:
        aliases = {len(args): 1}
        in_specs.append(pl.BlockSpec(memory_space=pl.ANY))
        args.append(prev_stack)
    kern = functools.partial(_hgrn_chunk_kernel, Lc=Lc, sub=sub, has_state=state is not None,
                             has_alias=prev_stack is not None)
    return pl.pallas_call(
        kern,
        out_shape=(jax.ShapeDtypeStruct((B, T, W), F32), jax.ShapeDtypeStruct((n_layers, B, H, K, K), F32)),
        grid=(B, nc),
        in_specs=in_specs,
        out_specs=(pl.BlockSpec((1, Lc, W), lambda b, c: (b, c, 0)), st_spec),
        scratch_shapes=[pltpu.VMEM((H, K, K), F32)],
        input_output_aliases=aliases,
        compiler_params=_params(("parallel", "arbitrary")),
        name="hgrn_scan",
    )(*args)


def _mlstm_chunk_kernel(*refs, Lc, pad, has_state, has_alias, single_chunk):
    q_ref, k_ref, v_ref, g_ref = refs[:4]
    pos = 4
    if has_state:
        c0_ref, n0_ref, m0_ref = refs[pos:pos + 3]
        pos += 3
    if has_alias:
        pos += 1
    h_ref, cout_ref, nout_ref, mout_ref = refs[pos:pos + 4]
    ci = pl.program_id(1)
    H = MLSTM_HEADS
    dh = q_ref.shape[2] // H
    carried = not single_chunk

    if carried:
        @pl.when(ci == 0)
        def _():
            if has_state:
                cout_ref[...] = c0_ref[...]
                nout_ref[...] = n0_ref[...]
                mout_ref[...] = m0_ref[...]
            else:
                cout_ref[...] = jnp.zeros_like(cout_ref)
                nout_ref[...] = jnp.zeros_like(nout_ref)
                mout_ref[...] = jnp.zeros_like(mout_ref)

    g = g_ref[0]
    t = ci * Lc + lax.broadcasted_iota(jnp.int32, (Lc, 1), 0)
    valid = t >= pad
    r = lax.broadcasted_iota(jnp.int32, (Lc, Lc), 0)
    c = lax.broadcasted_iota(jnp.int32, (Lc, Lc), 1)
    eye = r == c
    tril = c <= r
    for hd in range(H):
        sl = slice(hd * dh, (hd + 1) * dh)
        qb = q_ref[0, :, sl]
        kb = k_ref[0, :, sl]
        vb = v_ref[0, :, sl]
        q = qb.astype(F32)
        ks = kb.astype(F32)
        logi = g[:, hd:hd + 1]
        fpre = g[:, H + hd:H + hd + 1]
        logf = jnp.minimum(fpre, 0.0) - jnp.log1p(jnp.exp(-jnp.abs(fpre)))
        logf = jnp.where(valid, logf, 0.0)
        logi = jnp.where(valid, logi, NEG_BIG)
        logf_row = jnp.sum(jnp.where(eye, logf, 0.0), axis=0, keepdims=True)
        logi_row = jnp.sum(jnp.where(eye, logi, 0.0), axis=0, keepdims=True)
        b_col = jnp.sum(jnp.where(tril, logf_row, 0.0), axis=1, keepdims=True)
        b_row = jnp.sum(jnp.where(c >= r, logf, 0.0), axis=0, keepdims=True)
        if carried:
            C, n, m0 = cout_ref[0, hd], nout_ref[0, hd], mout_ref[0, hd][:, 0:1]
        elif has_state:
            C, n, m0 = c0_ref[0, hd], n0_ref[0, hd], m0_ref[0, hd][:, 0:1]
        else:
            C, n, m0 = jnp.zeros((dh, dh), F32), jnp.zeros((1, dh), F32), jnp.zeros((1, 1), F32)
        D = jnp.where(tril, b_col - b_row + logi_row, NEG_BIG)
        inter = b_col + m0
        m_t = jnp.maximum(inter, jnp.max(D, axis=1, keepdims=True))
        w_inter = jnp.exp(inter - m_t)
        s = _dot_nt(qb, kb) * jnp.exp(D - m_t)
        num = _dot(s.astype(BF16), vb) + w_inter * _dot(qb, C.astype(BF16))
        den = jnp.sum(s, axis=1, keepdims=True) + w_inter * jnp.sum(q * n, axis=1, keepdims=True)
        h_ref[0, :, sl] = num / jnp.maximum(jnp.abs(den), jnp.exp(-m_t))
        m_new = m_t[Lc - 1:Lc]
        bL = b_col[Lc - 1:Lc]
        carry = jnp.exp(bL + m0 - m_new)
        wj = jnp.exp(bL - b_col + logi - m_new)
        kw = ks * wj
        C_new = carry * C + _dot_tn(kw.astype(BF16), vb)
        n_new = carry * n + jnp.sum(kw, axis=0, keepdims=True)
        m_b = jnp.zeros((1, LANES), F32) + m_new
        cout_ref[0, hd] = C_new
        nout_ref[0, hd] = n_new
        mout_ref[0, hd] = m_b


def mlstm_scan(q3, k3, v3, g3, Lc, pad, state, prev_stack, layer, n_layers):
    B, T, D = q3.shape
    H = MLSTM_HEADS
    dh = D // H
    nc = T // Lc
    seq = lambda b, c: (b, c, 0)
    st = lambda b, c: (b, 0, 0, 0)
    st_l = lambda b, c: (layer, b, 0, 0, 0)
    in_specs = [pl.BlockSpec((1, Lc, D), seq), pl.BlockSpec((1, Lc, D), seq), pl.BlockSpec((1, Lc, D), seq),
                pl.BlockSpec((1, Lc, LANES), seq)]
    args = [q3, k3, v3, g3]
    if state is not None:
        in_specs += [pl.BlockSpec((None, 1, H, dh, dh), st_l), pl.BlockSpec((None, 1, H, 1, dh), st_l),
                     pl.BlockSpec((None, 1, H, 1, LANES), st_l)]
        args += list(state)
    aliases = {}
    if prev_stack is not None:
        aliases = {len(args): 1}
        in_specs.append(pl.BlockSpec(memory_space=pl.ANY))
        args.append(prev_stack)
    kern = functools.partial(_mlstm_chunk_kernel, Lc=Lc, pad=pad, has_state=state is not None,
                             has_alias=prev_stack is not None, single_chunk=nc == 1)
    return pl.pallas_call(
        kern,
        out_shape=(jax.ShapeDtypeStruct((B, T, D), F32), jax.ShapeDtypeStruct((n_layers, B, H, dh, dh), F32),
                   jax.ShapeDtypeStruct((B, H, 1, dh), F32), jax.ShapeDtypeStruct((B, H, 1, LANES), F32)),
        grid=(B, nc),
        in_specs=in_specs,
        out_specs=(pl.BlockSpec((1, Lc, D), seq), pl.BlockSpec((None, 1, H, dh, dh), st_l),
                   pl.BlockSpec((1, H, 1, dh), st), pl.BlockSpec((1, H, 1, LANES), st)),
        input_output_aliases=aliases,
        compiler_params=_params(("parallel", "arbitrary")),
        name="mlstm_scan",
    )(*args)


def _attn_prompt_kernel(q_ref, k_ref, kt_ref, o_ref, m_scr, l_scr, acc_scr, sa_scr, sb_scr, *, tq, tk, scale):
    i = pl.program_id(1)
    rows = MLA_HEADS * tq
    q = q_ref[...].reshape(rows, QK_WIDTH)
    m_scr[...] = jnp.full((rows, LANES), NEG_BIG, F32)
    l_scr[...] = jnp.zeros((rows, LANES), F32)
    acc_scr[...] = jnp.zeros((rows, KV_LORA), F32)
    last = ((i + 1) * tq - 1) // tk

    rsz = rows // ATTN_ROW_SPLITS

    scale2 = scale * LOG2E

    kloc = lax.broadcasted_iota(jnp.int32, (1, tk), 1)

    def scores(j, dst):
        dst[...] = _dot(q, kt_ref[:, pl.ds(pl.multiple_of(j * tk, tk), tk)]) * scale2

    def softmax_pv(j, src):
        start = pl.multiple_of(j * tk, tk)
        vb = k_ref[pl.ds(start, tk), 0:KV_LORA]
        for r in range(ATTN_ROW_SPLITS):
            rs = slice(r * rsz, (r + 1) * rsz)
            qrel = (i * tq - start) + (r * rsz + lax.broadcasted_iota(jnp.int32, (rsz, 1), 0)) % tq
            s = jnp.where(kloc <= qrel, src[rs, :], NEG_BIG)
            m_old = m_scr[rs, :]
            m_new = jnp.maximum(m_old, jnp.max(s, axis=1, keepdims=True))
            alpha = jnp.exp2(m_old - m_new)
            p = jnp.concatenate([jnp.exp2(s[:, c * LANES:(c + 1) * LANES] - m_new) for c in range(tk // LANES)],
                                axis=1)
            l_scr[rs, :] = alpha * l_scr[rs, :] + jnp.sum(p, axis=1, keepdims=True)
            pv = _dot(p.astype(BF16), vb)
            acc_scr[rs, :] = jnp.concatenate([alpha] * (KV_LORA // LANES), axis=1) * acc_scr[rs, :] + pv
            m_scr[rs, :] = m_new

    scores(0, sa_scr)
    pairs = last // 2

    def body(t, carry):
        j = 2 * t
        scores(j + 1, sb_scr)
        softmax_pv(j, sa_scr)
        scores(j + 2, sa_scr)
        softmax_pv(j + 1, sb_scr)
        return carry

    lax.fori_loop(0, pairs, body, 0)

    @pl.when(last % 2 == 0)
    def _():
        softmax_pv(last, sa_scr)

    @pl.when(last % 2 == 1)
    def _():
        scores(last, sb_scr)
        softmax_pv(last - 1, sa_scr)
        softmax_pv(last, sb_scr)

    inv = 1.0 / l_scr[...]
    out = acc_scr[...] * jnp.concatenate([inv] * (KV_LORA // LANES), axis=1)
    o_ref[...] = out.reshape(MLA_HEADS, tq, KV_LORA)


def attn_prompt(Q, K, KT, B, Tp, scale):
    tq, tk = Q_BLOCK, KV_BLOCK
    assert Tp % tq == 0
    nq = Tp // tq
    rows = MLA_HEADS * tq
    Tk = -(-Tp // tk) * tk
    Kp = jnp.pad(K.reshape(B, Tp, QK_WIDTH), ((0, 0), (0, Tk - Tp), (0, 0))).reshape(B * Tk, QK_WIDTH)
    KTp = jnp.pad(KT.reshape(QK_WIDTH, B, Tp), ((0, 0), (0, 0), (0, Tk - Tp))).reshape(QK_WIDTH, B * Tk)
    kern = functools.partial(_attn_prompt_kernel, tq=tq, tk=tk, scale=scale)
    return pl.pallas_call(
        kern,
        out_shape=jax.ShapeDtypeStruct((MLA_HEADS, B * Tp, KV_LORA), F32),
        grid=(B, nq),
        in_specs=[pl.BlockSpec((MLA_HEADS, tq, QK_WIDTH), lambda b, i: (0, b * nq + i, 0)),
                  pl.BlockSpec((Tk, QK_WIDTH), lambda b, i: (b, 0), pipeline_mode=pl.Buffered(1)),
                  pl.BlockSpec((QK_WIDTH, Tk), lambda b, i: (0, b), pipeline_mode=pl.Buffered(1))],
        out_specs=pl.BlockSpec((MLA_HEADS, tq, KV_LORA), lambda b, i: (0, b * nq + i, 0)),
        scratch_shapes=[pltpu.VMEM((rows, LANES), F32), pltpu.VMEM((rows, LANES), F32),
                        pltpu.VMEM((rows, KV_LORA), F32), pltpu.VMEM((rows, tk), F32), pltpu.VMEM((rows, tk), F32)],
        compiler_params=_params(("parallel", "arbitrary")),
        name="attn_prompt",
    )(Q, Kp, KTp)


def _attn_decode_kernel(pt_ref, q_ref, kn_ref, ckv_hbm, krt_hbm, o_ref, ckv_buf, krt_buf, sem,
                        *, layer, n_pages, page, group, L, scale):
    b = pl.program_id(0)
    nb = pl.num_programs(0)
    slot = b % 2
    rows = q_ref.shape[1]

    def page_copies(seq, sl, p):
        pid = pt_ref[seq, p]
        return (pltpu.make_async_copy(ckv_hbm.at[layer, pid], ckv_buf.at[sl, pl.ds(p * page, page)], sem.at[sl, 0]),
                pltpu.make_async_copy(krt_hbm.at[layer, pid], krt_buf.at[sl, :, pl.ds(p * page, page)], sem.at[sl, 1]))

    def start_gather(seq, sl):
        def body(p, carry):
            for cp in page_copies(seq, sl, p):
                cp.start()
            return carry
        lax.fori_loop(0, n_pages, body, 0)

    def wait_gather(seq, sl):
        def body(p, carry):
            for cp in page_copies(seq, sl, p):
                cp.wait()
            return carry
        lax.fori_loop(0, n_pages, body, 0)

    @pl.when(b == 0)
    def _():
        start_gather(0, 0)

    @pl.when(b + 1 < nb)
    def _():
        start_gather(b + 1, 1 - slot)

    q = q_ref[0]
    ql = q[:, 0:KV_LORA]
    qr = q[:, KV_LORA:KV_LORA + MLA_ROPE]
    kn = kn_ref[0]
    tpos = lax.broadcasted_iota(jnp.int32, (rows, L), 0) % L
    jpos = lax.broadcasted_iota(jnp.int32, (rows, L), 1)
    s_new = jnp.where(jpos <= tpos, _dot_nt(q, kn) * scale, NEG_BIG)

    wait_gather(b, slot)
    keys = group * page
    kcs, scores = [], []
    for g in range(n_pages // group):
        kc = ckv_buf[slot, g * keys:(g + 1) * keys, :].astype(BF16)
        krt = krt_buf[slot, :, g * keys:(g + 1) * keys].astype(BF16)
        kcs.append(kc)
        scores.append((_dot_nt(ql, kc) + _dot(qr, krt)) * scale)
    m_cur = scores[0]
    for s in scores[1:]:
        m_cur = jnp.maximum(m_cur, s)
    m = jnp.maximum(jnp.max(m_cur, axis=1, keepdims=True), jnp.max(s_new, axis=1, keepdims=True))
    p_new = jnp.exp(s_new - m)
    ps = [jnp.exp(s - m) for s in scores]
    psum = ps[0]
    for p in ps[1:]:
        psum = psum + p
    l = jnp.sum(psum, axis=1, keepdims=True) + jnp.sum(p_new, axis=1, keepdims=True)
    acc = _dot(p_new.astype(BF16), kn[:, 0:KV_LORA])
    for p, kc in zip(ps, kcs):
        acc = acc + _dot(p.astype(BF16), kc)
    o_ref[0] = acc / l


def attn_decode(Qs, Kn, cache_ckv, cache_krt, page_table, layer, scale):
    Bd, rows, _ = Qs.shape
    L = Kn.shape[1]
    n_pages = page_table.shape[1]
    page = cache_ckv.shape[2]
    group = min(PAGES_PER_STEP, n_pages)
    assert n_pages % group == 0
    kern = functools.partial(_attn_decode_kernel, layer=layer, n_pages=n_pages, page=page, group=group, L=L,
                             scale=scale)
    grid_spec = pltpu.PrefetchScalarGridSpec(
        num_scalar_prefetch=1,
        grid=(Bd,),
        in_specs=[pl.BlockSpec((1, rows, QK_WIDTH), lambda b, pt: (b, 0, 0)),
                  pl.BlockSpec((1, L, QK_WIDTH), lambda b, pt: (b, 0, 0)),
                  pl.BlockSpec(memory_space=pl.ANY), pl.BlockSpec(memory_space=pl.ANY)],
        out_specs=pl.BlockSpec((1, rows, KV_LORA), lambda b, pt: (b, 0, 0)),
        scratch_shapes=[pltpu.VMEM((2, n_pages * page, KV_LORA), F32),
                        pltpu.VMEM((2, MLA_ROPE, n_pages * page), F32),
                        pltpu.SemaphoreType.DMA((2, 2))],
    )
    return pl.pallas_call(
        kern,
        out_shape=jax.ShapeDtypeStruct((Bd, rows, KV_LORA), F32),
        grid_spec=grid_spec,
        compiler_params=_params(("arbitrary",)),
        name="attn_decode",
    )(page_table, Qs, Kn, cache_ckv, cache_krt)


def _rope_tables(pos):
    half = MLA_ROPE // 2
    inv = ROPE_THETA ** (-jnp.arange(half, dtype=F32) / half)
    ang = pos.astype(F32)[:, None] * inv[None, :]
    cos, sin = jnp.cos(ang), jnp.sin(ang)
    z = jnp.zeros((pos.shape[0], LANES - MLA_ROPE), F32)
    return jnp.concatenate([cos, cos, z], axis=1), jnp.concatenate([-sin, sin, z], axis=1)


def _expand_blockdiag(w):
    G = w.shape[0]
    per = BD_CHUNK // QKV_BLOCK
    w4 = w.reshape(G // per, per, QKV_BLOCK, QKV_BLOCK)
    eye = jnp.eye(per, dtype=w.dtype)
    dense = jnp.einsum('cgio,gh->cgiho', w4, eye)
    return dense.reshape(G // per, BD_CHUNK, BD_CHUNK).astype(BF16)


def _prep_even(e, ev_norm_w, ev_w_in, hgrn_norm_w, mla_q_norm_w, mla_w_q_b, mla_kv_norm_w, mla_w_uk, mla_w_uv, ev_w_out):
    W = HGRN_HEADS * HGRN_DK
    w = ev_w_in[e]
    o = 0
    parts = {}
    for name, width in (("q_a", W), ("f_a", W), ("i_a", W), ("g_a", W), ("cq", Q_LORA), ("ckv", KV_LORA),
                        ("kpe", MLA_ROPE), ("g_b", MLA_HEADS * MLA_V)):
        parts[name] = w[:, o:o + width]
        o += width
    zpad = jnp.zeros((w.shape[0], LANES - MLA_ROPE), w.dtype)
    w_in = jnp.concatenate([parts[n] for n in ("q_a", "f_a", "i_a", "g_a", "g_b", "cq", "ckv", "kpe")] + [zpad], axis=1)
    wqb = mla_w_q_b[e].reshape(Q_LORA, MLA_HEADS, MLA_NOPE + MLA_ROPE)
    nope = wqb[:, :, :MLA_NOPE].reshape(Q_LORA, MLA_HEADS * MLA_NOPE)
    pe = jnp.pad(wqb[:, :, MLA_NOPE:], ((0, 0), (0, 0), (0, LANES - MLA_ROPE))).reshape(Q_LORA, MLA_HEADS * LANES)
    return dict(
        nw=ev_norm_w[e][None, :], w_in=w_in.astype(BF16),
        hnw=hgrn_norm_w[e][None, :], qnw=mla_q_norm_w[e][None, :], kvnw=mla_kv_norm_w[e][None, :],
        wqb=jnp.concatenate([nope, pe], axis=1).astype(BF16),
        wuk_t=jnp.transpose(mla_w_uk[e], (1, 2, 0)).astype(BF16),
        wuv=jnp.transpose(mla_w_uv[e], (1, 0, 2)).astype(BF16),
        wout=ev_w_out[e].astype(BF16),
    )


def _prep_odd(e, od_norm_w, od_w_in, conv_w, conv_b, w_q, w_k, w_v, w_gate, b_gate, norm_w, skip, od_w_out):
    ng = w_gate.shape[2]
    return dict(
        nw=od_norm_w[e][None, :], w_in=od_w_in[e].astype(BF16),
        cw=conv_w[e], cb=conv_b[e][None, :],
        wq=_expand_blockdiag(w_q[e]), wk=_expand_blockdiag(w_k[e]), wv=_expand_blockdiag(w_v[e]),
        wg=jnp.pad(w_gate[e], ((0, 0), (0, LANES - ng))).astype(BF16),
        bg=jnp.pad(b_gate[e], (0, LANES - ng))[None, :],
        lnw=norm_w[e][None, :], skip=skip[e][None, :], wout=od_w_out[e].astype(BF16),
    )


def _even_layer(x, p, lb, tables, n_seq, seq_len, Lc, state, prev_stack, layer, n_layers, attend):
    W = HGRN_HEADS * HGRN_DK
    u = norm_matmul(x, p["nw"], p["w_in"])
    Q, K, KT, c, kr = mla_prep(u, *tables, p["qnw"], p["wqb"], p["wuk_t"], p["kvnw"], col0=5 * W)
    o_a, S_stack = hgrn_scan(u.reshape(n_seq, seq_len, u.shape[1]), lb, Lc, state, prev_stack, layer, n_layers)
    o_b = attend(Q, K, KT)
    x = even_out(x, o_a.reshape(x.shape[0], W), u, o_b, p["hnw"], p["wuv"], p["wout"], ga_blk=3, gb_blk=4)
    return x, c, kr, S_stack


def _odd_layer(x, p, n_seq, seq_len, Lc, pad, state, prev_stack, layer, n_layers, conv_buf):
    u = norm_matmul(x, p["nw"], p["w_in"])
    D = u.shape[1] // 2
    dh = D // MLSTM_HEADS
    u3 = u.reshape(n_seq, seq_len, 2 * D)
    if conv_buf is None:
        shifts = None
    else:
        xp = jnp.concatenate([conv_buf, u3[:, :, :D]], axis=1)
        shifts = [xp[:, CONV_W - 1 - s: CONV_W - 1 - s + seq_len].reshape(n_seq * seq_len, D) for s in (1, 2, 3)]
    xc, q, ks, v, g = odd_mid(u, shifts, seq_len, p["cw"], p["cb"], p["wq"], p["wk"], p["wv"], p["wg"], p["bg"],
                              k_scale=dh ** -0.5)
    r3 = lambda a: a.reshape(n_seq, seq_len, a.shape[1])
    h, C_stack, n, m = mlstm_scan(r3(q), r3(ks), r3(v), r3(g), Lc, pad, state, prev_stack, layer, n_layers)
    x = odd_out(x, h.reshape(x.shape[0], D), xc, u, p["lnw"], p["skip"], p["wout"])
    new_buf = u3[:, seq_len - (CONV_W - 1):, :D]
    return x, C_stack, n[:, :, 0, :], m[:, :, 0, 0], new_buf


def kernel(x_prompt, x_sample, cache_mla_ckv, cache_mla_krope, state_hgrn, state_mlstm_C, state_mlstm_n, state_mlstm_m, state_mlstm_conv, page_table, meta_tokens, ev_norm_w, ev_w_in, hgrn_lower_bounds, hgrn_norm_w, mla_q_norm_w, mla_w_q_b, mla_kv_norm_w, mla_w_uk, mla_w_uv, ev_w_out, od_norm_w, od_w_in, mlstm_conv_w, mlstm_conv_b, mlstm_w_q, mlstm_w_k, mlstm_w_v, mlstm_w_gate, mlstm_b_gate, mlstm_norm_w, mlstm_skip, od_w_out, final_norm_w):
    B, seq, Dm = x_prompt.shape
    Bd, Ls, _ = x_sample.shape
    NE, NO = ev_norm_w.shape[0], od_norm_w.shape[0]
    depth = NE + NO
    T = seq + N_META
    Tp = -(-T // ROW_ALIGN) * ROW_ALIGN
    pad = Tp - T
    assert seq % ROW_BLOCK == 0 and (pad + N_META) % ROW_BLOCK == 0 and (Bd * Ls) % ROW_BLOCK == 0
    assert Ls >= CONV_W - 1
    past_len = page_table.shape[1] * cache_mla_ckv.shape[2]
    scale = (MLA_NOPE + MLA_ROPE) ** -0.5
    cache_krt = jnp.swapaxes(cache_mla_krope, 2, 3)
    mlstm_state = (state_mlstm_C, state_mlstm_n[:, :, :, None, :],
                   jnp.broadcast_to(state_mlstm_m[:, :, :, None, None], state_mlstm_m.shape + (1, LANES)))

    hp = jnp.concatenate([jnp.zeros((B, pad, Dm), F32), jnp.broadcast_to(meta_tokens[None], (B, N_META, Dm)), x_prompt],
                         axis=1).reshape(B * Tp, Dm)
    hs = x_sample.reshape(Bd * Ls, Dm)
    cos_p, sin_p = _rope_tables(jnp.arange(Tp) - pad)
    bias_lane = (jnp.arange(LANES) == MLA_ROPE).astype(F32)[None, :]
    kbias_p = jnp.where(jnp.arange(Tp) < pad, NEG_BIG, 0.0).astype(F32)[:, None] * bias_lane
    tables_p = tuple(jnp.tile(t, (B, 1)) for t in (cos_p, sin_p, kbias_p))
    cos_s, sin_s = _rope_tables(past_len + jnp.arange(Ls))
    tables_s = (jnp.tile(cos_s, (Bd, 1)), jnp.tile(sin_s, (Bd, 1)), jnp.zeros((Bd * Ls, LANES), F32))
    sm = jax.nn.softmax(hgrn_lower_bounds.astype(F32), axis=0)
    lbs = jnp.cumsum(sm, axis=0) - sm[0]

    def attend_prompt(Q, K, KT):
        return attn_prompt(Q, K, KT, B, Tp, scale)

    def make_attend_sample(e):
        def attend(Q, K, KT):
            Qs = Q.reshape(MLA_HEADS, Bd, Ls, QK_WIDTH).transpose(1, 0, 2, 3).reshape(Bd, MLA_HEADS * Ls, QK_WIDTH)
            o = attn_decode(Qs, K.reshape(Bd, Ls, QK_WIDTH), cache_mla_ckv, cache_krt, page_table, e, scale)
            return o.reshape(Bd, MLA_HEADS, Ls, KV_LORA).transpose(1, 0, 2, 3).reshape(MLA_HEADS, Bd * Ls, KV_LORA)
        return attend

    outs_p = {k: [] for k in ("ckv", "kr", "n", "m", "cv")}
    outs_s = {k: [] for k in ("ckv", "kr", "n", "m", "cv")}
    S_p = S_s = C_p = C_s = None
    for l in range(depth):
        e = l // 2
        if l % 2 == 0:
            p = _prep_even(e, ev_norm_w, ev_w_in, hgrn_norm_w, mla_q_norm_w, mla_w_q_b, mla_kv_norm_w, mla_w_uk,
                           mla_w_uv, ev_w_out)
            lb = lbs[e][None, :]
            hp, c, kr, S_p = _even_layer(hp, p, lb, tables_p, B, Tp, HGRN_CHUNK, None, S_p, e, NE, attend_prompt)
            outs_p["ckv"].append(c.reshape(B, Tp, KV_LORA)[:, pad:])
            outs_p["kr"].append(kr.reshape(B, Tp, MLA_ROPE)[:, pad:])
            hs, c, kr, S_s = _even_layer(hs, p, lb, tables_s, Bd, Ls, Ls, state_hgrn, S_s, e, NE,
                                         make_attend_sample(e))
            outs_s["ckv"].append(c.reshape(Bd, Ls, KV_LORA))
            outs_s["kr"].append(kr.reshape(Bd, Ls, MLA_ROPE))
        else:
            p = _prep_odd(e, od_norm_w, od_w_in, mlstm_conv_w, mlstm_conv_b, mlstm_w_q, mlstm_w_k, mlstm_w_v,
                          mlstm_w_gate, mlstm_b_gate, mlstm_norm_w, mlstm_skip, od_w_out)
            hp, C_p, n, m, cv = _odd_layer(hp, p, B, Tp, MLSTM_CHUNK, pad, None, C_p, e, NO, None)
            for key, val in zip(("n", "m", "cv"), (n, m, cv)):
                outs_p[key].append(val)
            hs, C_s, n, m, cv = _odd_layer(hs, p, Bd, Ls, Ls, 0, mlstm_state, C_s, e, NO, state_mlstm_conv[e])
            for key, val in zip(("n", "m", "cv"), (n, m, cv)):
                outs_s[key].append(val)

    fw = final_norm_w[None, :]
    blocks_in = Tp // ROW_BLOCK
    skip_blocks = (pad + N_META) // ROW_BLOCK
    y_prompt = final_norm(hp, fw, skip_blocks, blocks_in, blocks_in - skip_blocks).reshape(B, seq, Dm)
    y_sample = final_norm(hs, fw, 0, (Bd * Ls) // ROW_BLOCK, (Bd * Ls) // ROW_BLOCK).reshape(Bd, Ls, Dm)
    st = lambda xs: jnp.stack(xs)
    return (y_prompt, y_sample,
            st(outs_p["ckv"]), st(outs_p["kr"]), S_p, C_p, st(outs_p["n"]), st(outs_p["m"]), st(outs_p["cv"]),
            st(outs_s["ckv"]), st(outs_s["kr"]), S_s, C_s, st(outs_s["n"]), st(outs_s["m"]), st(outs_s["cv"]))
```

```python
import functools

import jax
import jax.numpy as jnp
from jax import lax
from jax.experimental import pallas as pl
from jax.experimental.pallas import tpu as pltpu

F32 = jnp.float32
BF16 = jnp.bfloat16

N_META = 16
HGRN_HEADS = 8
HGRN_DK = 128
MLA_HEADS = 8
MLA_NOPE = 128
MLA_ROPE = 64
MLA_V = 128
Q_LORA = 512
KV_LORA = 256
ROPE_THETA = 10000.0
MLSTM_HEADS = 4
QKV_BLOCK = 4
CONV_W = 4
NEG_BIG = -1e30
RMS_EPS = 1e-6
LN_EPS = 1e-5
LOG2E = 1.4426950408889634

LANES = 128
ROW_ALIGN = 256
ROW_BLOCK = 256
QK_WIDTH = KV_LORA + LANES
Q_BLOCK = 128
KV_BLOCK = 512
ATTN_ROW_SPLITS = 1
HGRN_CHUNK = 128
MLSTM_CHUNK = 256
SUB = 32
PAGES_PER_STEP = 16
HGRN_SAFE_EXP = 60.0
BD_CHUNK = 256
VMEM_LIMIT = 56 * 1024 * 1024

_NT = (((1,), (1,)), ((), ()))
_TN = (((0,), (0,)), ((), ()))


def _dot(a, b):
    return jnp.dot(a, b, preferred_element_type=F32)


def _dot_nt(a, b):
    return lax.dot_general(a, b, _NT, preferred_element_type=F32)


def _dot_tn(a, b):
    return lax.dot_general(a, b, _TN, preferred_element_type=F32)


def _params(sem):
    return pltpu.CompilerParams(dimension_semantics=sem, vmem_limit_bytes=VMEM_LIMIT)


def _rms(x, w, eps=RMS_EPS):
    return x * lax.rsqrt(jnp.mean(x * x, axis=-1, keepdims=True) + eps) * w


def _silu(x):
    return x * jax.nn.sigmoid(x)


def _const_spec(shape):
    nd = len(shape)
    return pl.BlockSpec(shape, lambda *_: (0,) * nd, pipeline_mode=pl.Buffered(1))


def _norm_matmul_kernel(x_ref, nw_ref, w_ref, o_ref, *, chunks):
    xn = _rms(x_ref[...], nw_ref[...]).astype(BF16)
    for start, width in chunks:
        o_ref[:, start:start + width] = _dot(xn, w_ref[:, start:start + width])


def norm_matmul(x, nw, w):
    R, D = x.shape
    N = w.shape[1]
    tm = ROW_BLOCK
    chunks, s = [], 0
    while s < N:
        width = min(1024, N - s)
        chunks.append((s, width))
        s += width
    return pl.pallas_call(
        functools.partial(_norm_matmul_kernel, chunks=tuple(chunks)),
        out_shape=jax.ShapeDtypeStruct((R, N), F32),
        grid=(R // tm,),
        in_specs=[pl.BlockSpec((tm, D), lambda i: (i, 0)), _const_spec((1, D)), _const_spec((D, N))],
        out_specs=pl.BlockSpec((tm, N), lambda i: (i, 0)),
        compiler_params=_params(("parallel",)),
        name="norm_matmul",
    )(x, nw, w)


def _rope_tile(x, cos, sin):
    lane = lax.broadcasted_iota(jnp.int32, x.shape, 1)
    half = MLA_ROPE // 2
    partner = jnp.where(lane % MLA_ROPE < half, pltpu.roll(x, LANES - half, 1), pltpu.roll(x, half, 1))
    return x * cos + partner * sin


def _mla_prep_kernel(cq_ref, ckv_ref, kpe_ref, cos_ref, sin_ref, kbias_ref, qnw_ref, wqb_ref, wuk_ref, kvnw_ref,
                     q_ref, k_ref, kt_ref, c_ref, kr_ref):
    cos = cos_ref[...]
    sin = sin_ref[...]
    lane = lax.broadcasted_iota(jnp.int32, (1, LANES), 1)
    q_one = jnp.where(lane == MLA_ROPE, 1.0, 0.0)
    cqn = _rms(cq_ref[...], qnw_ref[...]).astype(BF16)
    qb = _dot(cqn, wqb_ref[...])
    nope_w = MLA_HEADS * MLA_NOPE
    for h in range(MLA_HEADS):
        qn = qb[:, h * MLA_NOPE:(h + 1) * MLA_NOPE].astype(BF16)
        q_ref[h, :, 0:KV_LORA] = _dot(qn, wuk_ref[h]).astype(BF16)
        qpe = qb[:, nope_w + h * LANES: nope_w + (h + 1) * LANES]
        q_ref[h, :, KV_LORA:QK_WIDTH] = (_rope_tile(qpe, cos, sin) + q_one).astype(BF16)
    c = _rms(ckv_ref[...], kvnw_ref[...])
    kr = _rope_tile(kpe_ref[...], cos, sin)
    c_ref[...] = c
    kr_ref[...] = kr[:, 0:MLA_ROPE]
    krb = kr + kbias_ref[...]
    k_ref[:, 0:KV_LORA] = c.astype(BF16)
    k_ref[:, KV_LORA:QK_WIDTH] = krb.astype(BF16)
    kt_ref[0:KV_LORA, :] = c.T.astype(BF16)
    kt_ref[KV_LORA:QK_WIDTH, :] = krb.T.astype(BF16)


def mla_prep(u, cos, sin, kbias, qnw, wqb, wuk_t, kvnw, col0):
    R = u.shape[0]
    tm = ROW_BLOCK
    cq_blk = col0 // Q_LORA
    ckv_blk = (col0 + Q_LORA) // KV_LORA
    kpe_blk = (col0 + Q_LORA + KV_LORA) // LANES
    return pl.pallas_call(
        _mla_prep_kernel,
        out_shape=(jax.ShapeDtypeStruct((MLA_HEADS, R, QK_WIDTH), BF16),
                   jax.ShapeDtypeStruct((R, QK_WIDTH), BF16),
                   jax.ShapeDtypeStruct((QK_WIDTH, R), BF16),
                   jax.ShapeDtypeStruct((R, KV_LORA), F32),
                   jax.ShapeDtypeStruct((R, MLA_ROPE), F32)),
        grid=(R // tm,),
        in_specs=[pl.BlockSpec((tm, Q_LORA), lambda i: (i, cq_blk)),
                  pl.BlockSpec((tm, KV_LORA), lambda i: (i, ckv_blk)),
                  pl.BlockSpec((tm, LANES), lambda i: (i, kpe_blk)),
                  pl.BlockSpec((tm, LANES), lambda i: (i, 0)),
                  pl.BlockSpec((tm, LANES), lambda i: (i, 0)),
                  pl.BlockSpec((tm, LANES), lambda i: (i, 0)),
                  _const_spec(qnw.shape), _const_spec(wqb.shape), _const_spec(wuk_t.shape), _const_spec(kvnw.shape)],
        out_specs=(pl.BlockSpec((MLA_HEADS, tm, QK_WIDTH), lambda i: (0, i, 0)),
                   pl.BlockSpec((tm, QK_WIDTH), lambda i: (i, 0)),
                   pl.BlockSpec((QK_WIDTH, tm), lambda i: (0, i)),
                   pl.BlockSpec((tm, KV_LORA), lambda i: (i, 0)),
                   pl.BlockSpec((tm, MLA_ROPE), lambda i: (i, 0))),
        compiler_params=_params(("parallel",)),
        name="mla_prep",
    )(u, u, u, cos, sin, kbias, qnw, wqb, wuk_t, kvnw)


def _even_out_kernel(x_ref, oa_ref, ga_ref, gb_ref, ob_ref, hnw_ref, wuv_ref, wout_ref, o_ref, y_scr):
    hw = HGRN_HEADS * HGRN_DK
    for h in range(HGRN_HEADS):
        sl = slice(h * HGRN_DK, (h + 1) * HGRN_DK)
        ya = _rms(oa_ref[:, sl], hnw_ref[:, sl])
        y_scr[:, sl] = (ya * _silu(ga_ref[:, sl])).astype(BF16)
    for h in range(MLA_HEADS):
        sl = slice(h * MLA_V, (h + 1) * MLA_V)
        yb = _dot(ob_ref[h].astype(BF16), wuv_ref[h])
        y_scr[:, hw + h * MLA_V: hw + (h + 1) * MLA_V] = (yb * _silu(gb_ref[:, sl])).astype(BF16)
    o_ref[...] = x_ref[...] + _dot(y_scr[...], wout_ref[...])


def even_out(x, o_a, u, o_b, hnw, wuv, wout, ga_blk, gb_blk):
    R, D = x.shape
    tm = ROW_BLOCK
    W = HGRN_HEADS * HGRN_DK
    return pl.pallas_call(
        _even_out_kernel,
        out_shape=jax.ShapeDtypeStruct((R, D), F32),
        grid=(R // tm,),
        in_specs=[pl.BlockSpec((tm, D), lambda i: (i, 0)),
                  pl.BlockSpec((tm, W), lambda i: (i, 0)),
                  pl.BlockSpec((tm, W), lambda i: (i, ga_blk)),
                  pl.BlockSpec((tm, W), lambda i: (i, gb_blk)),
                  pl.BlockSpec((MLA_HEADS, tm, KV_LORA), lambda i: (0, i, 0)),
                  _const_spec(hnw.shape), _const_spec(wuv.shape), _const_spec(wout.shape)],
        out_specs=pl.BlockSpec((tm, D), lambda i: (i, 0)),
        scratch_shapes=[pltpu.VMEM((tm, 2 * W), BF16)],
        compiler_params=_params(("parallel",)),
        name="even_out",
    )(x, o_a, u, u, o_b, hnw, wuv, wout)


def _odd_mid_kernel(*refs, tm, blocks_per_seq, shifted_inputs, k_scale):
    if shifted_inputs:
        x_ref, x1_ref, x2_ref, x3_ref = refs[:4]
        rest = refs[4:]
    else:
        x_ref, halo_ref = refs[:2]
        rest = refs[2:]
    cw_ref, cb_ref, wq_ref, wk_ref, wv_ref, wg_ref, bg_ref, xc_ref, q_ref, k_ref, v_ref, g_ref = rest[:12]
    x = x_ref[...]
    if shifted_inputs:
        shifted = [x1_ref[...], x2_ref[...], x3_ref[...]]
    else:
        xs = rest[12]
        first = (pl.program_id(0) % blocks_per_seq) == 0
        halo = halo_ref[...]
        xs[0:8, :] = jnp.where(first, jnp.zeros_like(halo), halo)
        xs[8:, :] = x
        shifted = [xs[pl.ds(8 - s, tm), :] for s in (1, 2, 3)]
    cw = cw_ref[...]
    pre = cb_ref[...] + x * cw[3:4] + shifted[0] * cw[2:3] + shifted[1] * cw[1:2] + shifted[2] * cw[0:1]
    xc = _silu(pre)
    xc_ref[...] = xc
    D = x.shape[1]
    g = jnp.zeros((x.shape[0], LANES), F32) + bg_ref[...]
    for c in range(D // BD_CHUNK):
        sl = slice(c * BD_CHUNK, (c + 1) * BD_CHUNK)
        xcb = xc[:, sl].astype(BF16)
        q = _dot(xcb, wq_ref[c])
        k = _dot(xcb, wk_ref[c])
        v = _dot(x[:, sl].astype(BF16), wv_ref[c])
        q_ref[:, sl] = q.astype(BF16)
        k_ref[:, sl] = (k * k_scale).astype(BF16)
        v_ref[:, sl] = v.astype(BF16)
        g = g + _dot(q.astype(BF16), wg_ref[c * BD_CHUNK:(c + 1) * BD_CHUNK, :])
        g = g + _dot(k.astype(BF16), wg_ref[D + c * BD_CHUNK: D + (c + 1) * BD_CHUNK, :])
        g = g + _dot(v.astype(BF16), wg_ref[2 * D + c * BD_CHUNK: 2 * D + (c + 1) * BD_CHUNK, :])
    g_ref[...] = g


def odd_mid(u, shifts, rows_per_seq, cw, cb, wq, wk, wv, wg, bg, k_scale):
    R = u.shape[0]
    D = u.shape[1] // 2
    tm = ROW_BLOCK
    row = lambda i: (i, 0)
    wspecs = [_const_spec(a.shape) for a in (cw, cb, wq, wk, wv, wg, bg)]
    outs = ((jax.ShapeDtypeStruct((R, D), F32),) + tuple(jax.ShapeDtypeStruct((R, D), BF16) for _ in range(3))
            + (jax.ShapeDtypeStruct((R, LANES), F32),))
    out_specs = tuple(pl.BlockSpec((tm, D), row) for _ in range(4)) + (pl.BlockSpec((tm, LANES), row),)
    if shifts is None:
        ins = [pl.BlockSpec((tm, D), row),
               pl.BlockSpec((8, D), lambda i: (jnp.maximum(i * (tm // 8) - 1, 0), 0))]
        args = (u, u)
        scratch = [pltpu.VMEM((tm + 8, D), F32)]
    else:
        ins = [pl.BlockSpec((tm, D), row)] * 4
        args = (u,) + tuple(shifts)
        scratch = []
    kern = functools.partial(_odd_mid_kernel, tm=tm, blocks_per_seq=rows_per_seq // tm,
                             shifted_inputs=shifts is not None, k_scale=k_scale)
    return pl.pallas_call(
        kern, out_shape=outs, grid=(R // tm,), in_specs=ins + wspecs, out_specs=out_specs,
        scratch_shapes=scratch, compiler_params=_params(("parallel",)), name="odd_mid",
    )(*args, cw, cb, wq, wk, wv, wg, bg)


def _odd_out_kernel(x_ref, h_ref, xc_ref, z_ref, nw_ref, skip_ref, wout_ref, o_ref, y_scr, *, dh):
    D = h_ref.shape[1]
    for hd in range(D // dh):
        sl = slice(hd * dh, (hd + 1) * dh)
        h = h_ref[:, sl]
        mu = jnp.mean(h, axis=-1, keepdims=True)
        hc = h - mu
        var = jnp.mean(hc * hc, axis=-1, keepdims=True)
        hn = hc * lax.rsqrt(var + LN_EPS) * nw_ref[:, sl]
        y = (hn + skip_ref[:, sl] * xc_ref[:, sl]) * _silu(z_ref[:, sl])
        y_scr[:, sl] = y.astype(BF16)
    o_ref[...] = x_ref[...] + _dot(y_scr[...], wout_ref[...])


def odd_out(x, h, xc, u, nw, skip, wout):
    R, Dm = x.shape
    D = h.shape[1]
    tm = ROW_BLOCK
    row = lambda i: (i, 0)
    return pl.pallas_call(
        functools.partial(_odd_out_kernel, dh=D // MLSTM_HEADS),
        out_shape=jax.ShapeDtypeStruct((R, Dm), F32),
        grid=(R // tm,),
        in_specs=[pl.BlockSpec((tm, Dm), row), pl.BlockSpec((tm, D), row), pl.BlockSpec((tm, D), row),
                  pl.BlockSpec((tm, D), lambda i: (i, 1)),
                  _const_spec(nw.shape), _const_spec(skip.shape), _const_spec(wout.shape)],
        out_specs=pl.BlockSpec((tm, Dm), row),
        scratch_shapes=[pltpu.VMEM((tm, D), BF16)],
        compiler_params=_params(("parallel",)),
        name="odd_out",
    )(x, h, xc, u, nw, skip, wout)


def _final_norm_kernel(x_ref, w_ref, o_ref):
    o_ref[...] = _rms(x_ref[...], w_ref[...])


def final_norm(x, w, skip_blocks, blocks_per_seq_in, blocks_per_seq_out):
    R, D = x.shape
    tm = ROW_BLOCK
    n_seq = R // (blocks_per_seq_in * tm)
    Ro = n_seq * blocks_per_seq_out * tm

    def in_map(i):
        return ((i // blocks_per_seq_out) * blocks_per_seq_in + skip_blocks + i % blocks_per_seq_out, 0)

    return pl.pallas_call(
        _final_norm_kernel,
        out_shape=jax.ShapeDtypeStruct((Ro, D), F32),
        grid=(Ro // tm,),
        in_specs=[pl.BlockSpec((tm, D), in_map), _const_spec(w.shape)],
        out_specs=pl.BlockSpec((tm, D), lambda i: (i, 0)),
        compiler_params=_params(("parallel",)),
        name="final_norm",
    )(x, w)


def _split3(x):
    hi = x.astype(BF16)
    r1 = x - hi.astype(F32)
    mid = r1.astype(BF16)
    lo = (r1 - mid.astype(F32)).astype(BF16)
    return hi, mid, lo


def _cumsum_rows(x):
    L = x.shape[0]
    r = lax.broadcasted_iota(jnp.int32, (L, L), 0)
    c = lax.broadcasted_iota(jnp.int32, (L, L), 1)
    tri = (c <= r).astype(BF16)
    hi, mid, lo = _split3(x)
    return _dot(tri, hi) + _dot(tri, mid) + _dot(tri, lo)


def _row_to_col(row):
    n = row.shape[1]
    r = lax.broadcasted_iota(jnp.int32, (n, n), 0)
    c = lax.broadcasted_iota(jnp.int32, (n, n), 1)
    return jnp.sum(jnp.where(r == c, row, 0.0), axis=1, keepdims=True)


def _hgrn_chunk_kernel(*refs, Lc, sub, has_state, has_alias):
    qa_ref, fa_ref, ia_ref, lb_ref = refs[:4]
    pos = 4
    s0_ref = None
    if has_state:
        s0_ref = refs[pos]
        pos += 1
    if has_alias:
        pos += 1
    o_ref, sout_ref, s_scr = refs[pos:pos + 3]
    H, K = HGRN_HEADS, HGRN_DK

    @pl.when(pl.program_id(1) == 0)
    def _():
        s_scr[...] = s0_ref[0] if has_state else jnp.zeros_like(s_scr)

    lb = lb_ref[...]
    qa = qa_ref[0]
    fa = fa_ref[0]
    q = _silu(qa)
    logf = jnp.log(lb + (1.0 - lb) * jax.nn.sigmoid(fa))
    k = (1.0 - lb) * jax.nn.sigmoid(-fa)
    vb = ia_ref[0].astype(BF16)
    G = _cumsum_rows(logf)
    qg = q * jnp.exp(G)
    nblk = Lc // sub
    worst = -G[sub - 1:sub]
    for i in range(1, nblk):
        worst = jnp.maximum(worst, G[i * sub - 1:i * sub] - G[(i + 1) * sub - 1:(i + 1) * sub])
    safe = jnp.max(worst) <= HGRN_SAFE_EXP
    row = lax.broadcasted_iota(jnp.int32, (sub, Lc), 0)
    col = lax.broadcasted_iota(jnp.int32, (sub, Lc), 1)
    tr = lax.broadcasted_iota(jnp.int32, (sub, sub), 0)
    tc = lax.broadcasted_iota(jnp.int32, (sub, sub), 1)

    def inter(h):
        sl = slice(h * K, (h + 1) * K)
        return sl, _dot(qg[:, sl].astype(BF16), s_scr[h].astype(BF16))

    @pl.when(safe)
    def _():
        for h in range(H):
            sl, o = inter(h)
            Gh, qh, kh = G[:, sl], q[:, sl], k[:, sl]
            a_rows = []
            for i in range(nblk):
                lo = i * sub
                r = Gh[lo - 1:lo] if i > 0 else jnp.zeros((1, K), F32)
                qd = (qh[lo:lo + sub] * jnp.exp(Gh[lo:lo + sub] - r)).astype(BF16)
                kd = (kh * jnp.exp(jnp.minimum(r - Gh, HGRN_SAFE_EXP))).astype(BF16)
                a_rows.append(jnp.where(col <= lo + row, _dot_nt(qd, kd), 0.0))
            a = a_rows[0] if nblk == 1 else jnp.concatenate(a_rows, axis=0)
            o_ref[0, :, sl] = o + _dot(a.astype(BF16), vb[:, sl])

    @pl.when(jnp.logical_not(safe))
    def _():
        for h in range(H):
            sl, o_int = inter(h)
            Gh, qh, kh, vh = G[:, sl], q[:, sl], k[:, sl], vb[:, sl]
            for i in range(nblk):
                lo = i * sub
                Gi, qi, ki = Gh[lo:lo + sub], qh[lo:lo + sub], kh[lo:lo + sub]
                e = jnp.exp(jnp.minimum(Gi[:, None, :] - Gi[None, :, :], 0.0))
                ad = jnp.sum(qi[:, None, :] * ki[None, :, :] * e, axis=-1)
                ad = jnp.where(tc <= tr, ad, 0.0)
                o = o_int[lo:lo + sub] + _dot(ad.astype(BF16), vh[lo:lo + sub])
                if i > 0:
                    r = Gh[lo - 1:lo]
                    qd = (qi * jnp.exp(Gi - r)).astype(BF16)
                    kd = (kh[:lo] * jnp.exp(r - Gh[:lo])).astype(BF16)
                    o = o + _dot(_dot_nt(qd, kd).astype(BF16), vh[:lo])
                o_ref[0, lo:lo + sub, sl] = o

    GL = G[Lc - 1:Lc]
    kd_all = (k * jnp.exp(GL - G)).astype(BF16)
    eGL = jnp.exp(GL)
    for h in range(H):
        sl = slice(h * K, (h + 1) * K)
        S_new = _row_to_col(eGL[:, sl]) * s_scr[h] + _dot_tn(kd_all[:, sl], vb[:, sl])
        s_scr[h] = S_new
        sout_ref[0, h] = S_new


def hgrn_scan(u3, lb, Lc, state, prev_stack, layer, n_layers):
    B, T, _ = u3.shape
    H, K = HGRN_HEADS, HGRN_DK
    W = H * K
    nc = T // Lc
    sub = min(SUB, Lc)
    st_spec = pl.BlockSpec((None, 1, H, K, K), lambda b, c: (layer, b, 0, 0, 0))
    in_specs = [pl.BlockSpec((1, Lc, W), lambda b, c: (b, c, 0)),
                pl.BlockSpec((1, Lc, W), lambda b, c: (b, c, 1)),
                pl.BlockSpec((1, Lc, W), lambda b, c: (b, c, 2)),
                pl.BlockSpec((1, W), lambda b, c: (0, 0))]
    args = [u3, u3, u3, lb]
    if state is not None:
        in_specs.append(st_spec)
        args.append(state)
    aliases = {}
    if prev_stack is not None:
        aliases = {len(args): 1}
        in_specs.append(pl.BlockSpec(memory_space=pl.ANY))
        args.append(prev_stack)
    kern = functools.partial(_hgrn_chunk_kernel, Lc=Lc, sub=sub, has_state=state is not None,
                             has_alias=prev_stack is not None)
    return pl.pallas_call(
        kern,
        out_shape=(jax.ShapeDtypeStruct((B, T, W), F32), jax.ShapeDtypeStruct((n_layers, B, H, K, K), F32)),
        grid=(B, nc),
        in_specs=in_specs,
        out_specs=(pl.BlockSpec((1, Lc, W), lambda b, c: (b, c, 0)), st_spec),
        scratch_shapes=[pltpu.VMEM((H, K, K), F32)],
        input_output_aliases=aliases,
        compiler_params=_params(("parallel", "arbitrary")),
        name="hgrn_scan",
    )(*args)


def _mlstm_chunk_kernel(*refs, Lc, pad, has_state, has_alias, single_chunk):
    q_ref, k_ref, v_ref, g_ref = refs[:4]
    pos = 4
    if has_state:
        c0_ref, n0_ref, m0_ref = refs[pos:pos + 3]
        pos += 3
    if has_alias:
        pos += 1
    h_ref, cout_ref, nout_ref, mout_ref = refs[pos:pos + 4]
    ci = pl.program_id(1)
    H = MLSTM_HEADS
    dh = q_ref.shape[2] // H
    carried = not single_chunk

    if carried:
        @pl.when(ci == 0)
        def _():
            if has_state:
                cout_ref[...] = c0_ref[...]
                nout_ref[...] = n0_ref[...]
                mout_ref[...] = m0_ref[...]
            else:
                cout_ref[...] = jnp.zeros_like(cout_ref)
                nout_ref[...] = jnp.zeros_like(nout_ref)
                mout_ref[...] = jnp.zeros_like(mout_ref)

    g = g_ref[0]
    t = ci * Lc + lax.broadcasted_iota(jnp.int32, (Lc, 1), 0)
    valid = t >= pad
    r = lax.broadcasted_iota(jnp.int32, (Lc, Lc), 0)
    c = lax.broadcasted_iota(jnp.int32, (Lc, Lc), 1)
    eye = r == c
    tril = c <= r
    for hd in range(H):
        sl = slice(hd * dh, (hd + 1) * dh)
        qb = q_ref[0, :, sl]
        kb = k_ref[0, :, sl]
        vb = v_ref[0, :, sl]
        q = qb.astype(F32)
        ks = kb.astype(F32)
        logi = g[:, hd:hd + 1]
        fpre = g[:, H + hd:H + hd + 1]
        logf = jnp.minimum(fpre, 0.0) - jnp.log1p(jnp.exp(-jnp.abs(fpre)))
        logf = jnp.where(valid, logf, 0.0)
        logi = jnp.where(valid, logi, NEG_BIG)
        logf_row = jnp.sum(jnp.where(eye, logf, 0.0), axis=0, keepdims=True)
        logi_row = jnp.sum(jnp.where(eye, logi, 0.0), axis=0, keepdims=True)
        b_col = jnp.sum(jnp.where(tril, logf_row, 0.0), axis=1, keepdims=True)
        b_row = jnp.sum(jnp.where(c >= r, logf, 0.0), axis=0, keepdims=True)
        if carried:
            C, n, m0 = cout_ref[0, hd], nout_ref[0, hd], mout_ref[0, hd][:, 0:1]
        elif has_state:
            C, n, m0 = c0_ref[0, hd], n0_ref[0, hd], m0_ref[0, hd][:, 0:1]
        else:
            C, n, m0 = jnp.zeros((dh, dh), F32), jnp.zeros((1, dh), F32), jnp.zeros((1, 1), F32)
        D = jnp.where(tril, b_col - b_row + logi_row, NEG_BIG)
        inter = b_col + m0
        m_t = jnp.maximum(inter, jnp.max(D, axis=1, keepdims=True))
        w_inter = jnp.exp(inter - m_t)
        s = _dot_nt(qb, kb) * jnp.exp(D - m_t)
        num = _dot(s.astype(BF16), vb) + w_inter * _dot(qb, C.astype(BF16))
        den = jnp.sum(s, axis=1, keepdims=True) + w_inter * jnp.sum(q * n, axis=1, keepdims=True)
        h_ref[0, :, sl] = num / jnp.maximum(jnp.abs(den), jnp.exp(-m_t))
        m_new = m_t[Lc - 1:Lc]
        bL = b_col[Lc - 1:Lc]
        carry = jnp.exp(bL + m0 - m_new)
        wj = jnp.exp(bL - b_col + logi - m_new)
        kw = ks * wj
        C_new = carry * C + _dot_tn(kw.astype(BF16), vb)
        n_new = carry * n + jnp.sum(kw, axis=0, keepdims=True)
        m_b = jnp.zeros((1, LANES), F32) + m_new
        cout_ref[0, hd] = C_new
        nout_ref[0, hd] = n_new
        mout_ref[0, hd] = m_b


def mlstm_scan(q3, k3, v3, g3, Lc, pad, state, prev_stack, layer, n_layers):
    B, T, D = q3.shape
    H = MLSTM_HEADS
    dh = D // H
    nc = T // Lc
    seq = lambda b, c: (b, c, 0)
    st = lambda b, c: (b, 0, 0, 0)
    st_l = lambda b, c: (layer, b, 0, 0, 0)
    in_specs = [pl.BlockSpec((1, Lc, D), seq), pl.BlockSpec((1, Lc, D), seq), pl.BlockSpec((1, Lc, D), seq),
                pl.BlockSpec((1, Lc, LANES), seq)]
    args = [q3, k3, v3, g3]
    if state is not None:
        in_specs += [pl.BlockSpec((None, 1, H, dh, dh), st_l), pl.BlockSpec((None, 1, H, 1, dh), st_l),
                     pl.BlockSpec((None, 1, H, 1, LANES), st_l)]
        args += list(state)
    aliases = {}
    if prev_stack is not None:
        aliases = {len(args): 1}
        in_specs.append(pl.BlockSpec(memory_space=pl.ANY))
        args.append(prev_stack)
    kern = functools.partial(_mlstm_chunk_kernel, Lc=Lc, pad=pad, has_state=state is not None,
                             has_alias=prev_stack is not None, single_chunk=nc == 1)
    return pl.pallas_call(
        kern,
        out_shape=(jax.ShapeDtypeStruct((B, T, D), F32), jax.ShapeDtypeStruct((n_layers, B, H, dh, dh), F32),
                   jax.ShapeDtypeStruct((B, H, 1, dh), F32), jax.ShapeDtypeStruct((B, H, 1, LANES), F32)),
        grid=(B, nc),
        in_specs=in_specs,
        out_specs=(pl.BlockSpec((1, Lc, D), seq), pl.BlockSpec((None, 1, H, dh, dh), st_l),
                   pl.BlockSpec((1, H, 1, dh), st), pl.BlockSpec((1, H, 1, LANES), st)),
        input_output_aliases=aliases,
        compiler_params=_params(("parallel", "arbitrary")),
        name="mlstm_scan",
    )(*args)


def _attn_prompt_kernel(q_ref, k_ref, kt_ref, o_ref, m_scr, l_scr, acc_scr, sa_scr, sb_scr, *, tq, tk, scale):
    i = pl.program_id(1)
    rows = MLA_HEADS * tq
    q = q_ref[...].reshape(rows, QK_WIDTH)
    m_scr[...] = jnp.full((rows, LANES), NEG_BIG, F32)
    l_scr[...] = jnp.zeros((rows, LANES), F32)
    acc_scr[...] = jnp.zeros((rows, KV_LORA), F32)
    last = ((i + 1) * tq - 1) // tk

    rsz = rows // ATTN_ROW_SPLITS

    scale2 = scale * LOG2E

    kloc = lax.broadcasted_iota(jnp.int32, (1, tk), 1)

    def scores(j, dst):
        dst[...] = _dot(q, kt_ref[:, pl.ds(pl.multiple_of(j * tk, tk), tk)]) * scale2

    def softmax_pv(j, src, causal):
        start = pl.multiple_of(j * tk, tk)
        vb = k_ref[pl.ds(start, tk), 0:KV_LORA]
        for r in range(ATTN_ROW_SPLITS):
            rs = slice(r * rsz, (r + 1) * rsz)
            s = src[rs, :]
            if causal:
                qrel = (i * tq - start) + (r * rsz + lax.broadcasted_iota(jnp.int32, (rsz, 1), 0)) % tq
                s = jnp.where(kloc <= qrel, s, NEG_BIG)
            m_old = m_scr[rs, :]
            m_new = jnp.maximum(m_old, jnp.max(s, axis=1, keepdims=True))
            alpha = jnp.exp2(m_old - m_new)
            p = jnp.concatenate([jnp.exp2(s[:, c * LANES:(c + 1) * LANES] - m_new) for c in range(tk // LANES)],
                                axis=1)
            l_scr[rs, :] = alpha * l_scr[rs, :] + jnp.sum(p, axis=1, keepdims=True)
            pv = _dot(p.astype(BF16), vb)
            acc_scr[rs, :] = jnp.concatenate([alpha] * (KV_LORA // LANES), axis=1) * acc_scr[rs, :] + pv
            m_scr[rs, :] = m_new

    scores(0, sa_scr)
    pairs = last // 2

    def body(t, carry):
        j = 2 * t
        scores(j + 1, sb_scr)
        softmax_pv(j, sa_scr, False)
        scores(j + 2, sa_scr)
        softmax_pv(j + 1, sb_scr, False)
        return carry

    lax.fori_loop(0, pairs, body, 0)

    @pl.when(last % 2 == 0)
    def _():
        softmax_pv(last, sa_scr, True)

    @pl.when(last % 2 == 1)
    def _():
        scores(last, sb_scr)
        softmax_pv(last - 1, sa_scr, False)
        softmax_pv(last, sb_scr, True)

    inv = 1.0 / l_scr[...]
    out = acc_scr[...] * jnp.concatenate([inv] * (KV_LORA // LANES), axis=1)
    o_ref[...] = out.reshape(MLA_HEADS, tq, KV_LORA)


def attn_prompt(Q, K, KT, B, Tp, scale):
    tq, tk = Q_BLOCK, KV_BLOCK
    assert Tp % tq == 0
    nq = Tp // tq
    rows = MLA_HEADS * tq
    Tk = -(-Tp // tk) * tk
    if Tk > Tp:
        zr = jnp.zeros((Tk - Tp, QK_WIDTH), K.dtype)
        zc = jnp.zeros((QK_WIDTH, Tk - Tp), KT.dtype)
        Kp = jnp.concatenate([a for b in range(B) for a in (K[b * Tp:(b + 1) * Tp], zr)], axis=0)
        KTp = jnp.concatenate([a for b in range(B) for a in (KT[:, b * Tp:(b + 1) * Tp], zc)], axis=1)
    else:
        Kp, KTp = K, KT
    kern = functools.partial(_attn_prompt_kernel, tq=tq, tk=tk, scale=scale)
    return pl.pallas_call(
        kern,
        out_shape=jax.ShapeDtypeStruct((MLA_HEADS, B * Tp, KV_LORA), F32),
        grid=(B, nq),
        in_specs=[pl.BlockSpec((MLA_HEADS, tq, QK_WIDTH), lambda b, i: (0, b * nq + i, 0)),
                  pl.BlockSpec((Tk, QK_WIDTH), lambda b, i: (b, 0), pipeline_mode=pl.Buffered(1)),
                  pl.BlockSpec((QK_WIDTH, Tk), lambda b, i: (0, b), pipeline_mode=pl.Buffered(1))],
        out_specs=pl.BlockSpec((MLA_HEADS, tq, KV_LORA), lambda b, i: (0, b * nq + i, 0)),
        scratch_shapes=[pltpu.VMEM((rows, LANES), F32), pltpu.VMEM((rows, LANES), F32),
                        pltpu.VMEM((rows, KV_LORA), F32), pltpu.VMEM((rows, tk), F32), pltpu.VMEM((rows, tk), F32)],
        compiler_params=_params(("parallel", "arbitrary")),
        name="attn_prompt",
    )(Q, Kp, KTp)


def _attn_decode_kernel(pt_ref, q_ref, kn_ref, ckv_hbm, krt_hbm, o_ref, ckv_buf, krt_buf, sem,
                        *, layer, n_pages, page, group, L, scale):
    b = pl.program_id(0)
    nb = pl.num_programs(0)
    slot = b % 2
    rows = q_ref.shape[1]

    def page_copies(seq, sl, p):
        pid = pt_ref[seq, p]
        return (pltpu.make_async_copy(ckv_hbm.at[layer, pid], ckv_buf.at[sl, pl.ds(p * page, page)], sem.at[sl, 0]),
                pltpu.make_async_copy(krt_hbm.at[layer, pid], krt_buf.at[sl, :, pl.ds(p * page, page)], sem.at[sl, 1]))

    def start_gather(seq, sl):
        def body(p, carry):
            for cp in page_copies(seq, sl, p):
                cp.start()
            return carry
        lax.fori_loop(0, n_pages, body, 0)

    def wait_gather(seq, sl):
        def body(p, carry):
            for cp in page_copies(seq, sl, p):
                cp.wait()
            return carry
        lax.fori_loop(0, n_pages, body, 0)

    @pl.when(b == 0)
    def _():
        start_gather(0, 0)

    @pl.when(b + 1 < nb)
    def _():
        start_gather(b + 1, 1 - slot)

    q = q_ref[0]
    ql = q[:, 0:KV_LORA]
    qr = q[:, KV_LORA:KV_LORA + MLA_ROPE]
    kn = kn_ref[0]
    tpos = lax.broadcasted_iota(jnp.int32, (rows, L), 0) % L
    jpos = lax.broadcasted_iota(jnp.int32, (rows, L), 1)
    s_new = jnp.where(jpos <= tpos, _dot_nt(q, kn) * scale, NEG_BIG)

    wait_gather(b, slot)
    keys = group * page
    kcs, scores = [], []
    for g in range(n_pages // group):
        kc = ckv_buf[slot, g * keys:(g + 1) * keys, :].astype(BF16)
        krt = krt_buf[slot, :, g * keys:(g + 1) * keys].astype(BF16)
        kcs.append(kc)
        scores.append((_dot_nt(ql, kc) + _dot(qr, krt)) * scale)
    m_cur = scores[0]
    for s in scores[1:]:
        m_cur = jnp.maximum(m_cur, s)
    m = jnp.maximum(jnp.max(m_cur, axis=1, keepdims=True), jnp.max(s_new, axis=1, keepdims=True))
    p_new = jnp.exp(s_new - m)
    ps = [jnp.exp(s - m) for s in scores]
    psum = ps[0]
    for p in ps[1:]:
        psum = psum + p
    l = jnp.sum(psum, axis=1, keepdims=True) + jnp.sum(p_new, axis=1, keepdims=True)
    acc = _dot(p_new.astype(BF16), kn[:, 0:KV_LORA])
    for p, kc in zip(ps, kcs):
        acc = acc + _dot(p.astype(BF16), kc)
    o_ref[0] = acc / l


def attn_decode(Qs, Kn, cache_ckv, cache_krt, page_table, layer, scale):
    Bd, rows, _ = Qs.shape
    L = Kn.shape[1]
    n_pages = page_table.shape[1]
    page = cache_ckv.shape[2]
    group = min(PAGES_PER_STEP, n_pages)
    assert n_pages % group == 0
    kern = functools.partial(_attn_decode_kernel, layer=layer, n_pages=n_pages, page=page, group=group, L=L,
                             scale=scale)
    grid_spec = pltpu.PrefetchScalarGridSpec(
        num_scalar_prefetch=1,
        grid=(Bd,),
        in_specs=[pl.BlockSpec((1, rows, QK_WIDTH), lambda b, pt: (b, 0, 0)),
                  pl.BlockSpec((1, L, QK_WIDTH), lambda b, pt: (b, 0, 0)),
                  pl.BlockSpec(memory_space=pl.ANY), pl.BlockSpec(memory_space=pl.ANY)],
        out_specs=pl.BlockSpec((1, rows, KV_LORA), lambda b, pt: (b, 0, 0)),
        scratch_shapes=[pltpu.VMEM((2, n_pages * page, KV_LORA), F32),
                        pltpu.VMEM((2, MLA_ROPE, n_pages * page), F32),
                        pltpu.SemaphoreType.DMA((2, 2))],
    )
    return pl.pallas_call(
        kern,
        out_shape=jax.ShapeDtypeStruct((Bd, rows, KV_LORA), F32),
        grid_spec=grid_spec,
        compiler_params=_params(("arbitrary",)),
        name="attn_decode",
    )(page_table, Qs, Kn, cache_ckv, cache_krt)


def _rope_tables(pos):
    half = MLA_ROPE // 2
    inv = ROPE_THETA ** (-jnp.arange(half, dtype=F32) / half)
    ang = pos.astype(F32)[:, None] * inv[None, :]
    cos, sin = jnp.cos(ang), jnp.sin(ang)
    z = jnp.zeros((pos.shape[0], LANES - MLA_ROPE), F32)
    return jnp.concatenate([cos, cos, z], axis=1), jnp.concatenate([-sin, sin, z], axis=1)


def _expand_blockdiag(w):
    G = w.shape[0]
    per = BD_CHUNK // QKV_BLOCK
    rows = w.reshape(G // per, BD_CHUNK, QKV_BLOCK)
    tiled = jnp.tile(rows, (1, 1, per))
    idx = jnp.arange(BD_CHUNK) // QKV_BLOCK
    return jnp.where(idx[:, None] == idx[None, :], tiled, 0.0).astype(BF16)


def _prep_even(e, ev_norm_w, ev_w_in, hgrn_norm_w, mla_q_norm_w, mla_w_q_b, mla_kv_norm_w, mla_w_uk, mla_w_uv, ev_w_out):
    W = HGRN_HEADS * HGRN_DK
    w = ev_w_in[e]
    o = 0
    parts = {}
    for name, width in (("q_a", W), ("f_a", W), ("i_a", W), ("g_a", W), ("cq", Q_LORA), ("ckv", KV_LORA),
                        ("kpe", MLA_ROPE), ("g_b", MLA_HEADS * MLA_V)):
        parts[name] = w[:, o:o + width]
        o += width
    zpad = jnp.zeros((w.shape[0], LANES - MLA_ROPE), w.dtype)
    w_in = jnp.concatenate([parts[n] for n in ("q_a", "f_a", "i_a", "g_a", "g_b", "cq", "ckv", "kpe")] + [zpad], axis=1)
    wqb = mla_w_q_b[e].reshape(Q_LORA, MLA_HEADS, MLA_NOPE + MLA_ROPE)
    nope = wqb[:, :, :MLA_NOPE].reshape(Q_LORA, MLA_HEADS * MLA_NOPE)
    pe = jnp.pad(wqb[:, :, MLA_NOPE:], ((0, 0), (0, 0), (0, LANES - MLA_ROPE))).reshape(Q_LORA, MLA_HEADS * LANES)
    return dict(
        nw=ev_norm_w[e][None, :], w_in=w_in.astype(BF16),
        hnw=hgrn_norm_w[e][None, :], qnw=mla_q_norm_w[e][None, :], kvnw=mla_kv_norm_w[e][None, :],
        wqb=jnp.concatenate([nope, pe], axis=1).astype(BF16),
        wuk_t=jnp.transpose(mla_w_uk[e], (1, 2, 0)).astype(BF16),
        wuv=jnp.transpose(mla_w_uv[e], (1, 0, 2)).astype(BF16),
        wout=ev_w_out[e].astype(BF16),
    )


def _prep_odd(e, od_norm_w, od_w_in, conv_w, conv_b, w_q, w_k, w_v, w_gate, b_gate, norm_w, skip, od_w_out):
    ng = w_gate.shape[2]
    return dict(
        nw=od_norm_w[e][None, :], w_in=od_w_in[e].astype(BF16),
        cw=conv_w[e], cb=conv_b[e][None, :],
        wq=_expand_blockdiag(w_q[e]), wk=_expand_blockdiag(w_k[e]), wv=_expand_blockdiag(w_v[e]),
        wg=jnp.pad(w_gate[e], ((0, 0), (0, LANES - ng))).astype(BF16),
        bg=jnp.pad(b_gate[e], (0, LANES - ng))[None, :],
        lnw=norm_w[e][None, :], skip=skip[e][None, :], wout=od_w_out[e].astype(BF16),
    )


def _even_layer(x, p, lb, tables, n_seq, seq_len, Lc, state, prev_stack, layer, n_layers, attend):
    W = HGRN_HEADS * HGRN_DK
    u = norm_matmul(x, p["nw"], p["w_in"])
    Q, K, KT, c, kr = mla_prep(u, *tables, p["qnw"], p["wqb"], p["wuk_t"], p["kvnw"], col0=5 * W)
    o_a, S_stack = hgrn_scan(u.reshape(n_seq, seq_len, u.shape[1]), lb, Lc, state, prev_stack, layer, n_layers)
    o_b = attend(Q, K, KT)
    x = even_out(x, o_a.reshape(x.shape[0], W), u, o_b, p["hnw"], p["wuv"], p["wout"], ga_blk=3, gb_blk=4)
    return x, c, kr, S_stack


def _odd_layer(x, p, n_seq, seq_len, Lc, pad, state, prev_stack, layer, n_layers, conv_buf):
    u = norm_matmul(x, p["nw"], p["w_in"])
    D = u.shape[1] // 2
    dh = D // MLSTM_HEADS
    u3 = u.reshape(n_seq, seq_len, 2 * D)
    if conv_buf is None:
        shifts = None
    else:
        xp = jnp.concatenate([conv_buf, u3[:, :, :D]], axis=1)
        shifts = [xp[:, CONV_W - 1 - s: CONV_W - 1 - s + seq_len].reshape(n_seq * seq_len, D) for s in (1, 2, 3)]
    xc, q, ks, v, g = odd_mid(u, shifts, seq_len, p["cw"], p["cb"], p["wq"], p["wk"], p["wv"], p["wg"], p["bg"],
                              k_scale=dh ** -0.5)
    r3 = lambda a: a.reshape(n_seq, seq_len, a.shape[1])
    h, C_stack, n, m = mlstm_scan(r3(q), r3(ks), r3(v), r3(g), Lc, pad, state, prev_stack, layer, n_layers)
    x = odd_out(x, h.reshape(x.shape[0], D), xc, u, p["lnw"], p["skip"], p["wout"])
    new_buf = u3[:, seq_len - (CONV_W - 1):, :D]
    return x, C_stack, n[:, :, 0, :], m[:, :, 0, 0], new_buf


def kernel(x_prompt, x_sample, cache_mla_ckv, cache_mla_krope, state_hgrn, state_mlstm_C, state_mlstm_n, state_mlstm_m, state_mlstm_conv, page_table, meta_tokens, ev_norm_w, ev_w_in, hgrn_lower_bounds, hgrn_norm_w, mla_q_norm_w, mla_w_q_b, mla_kv_norm_w, mla_w_uk, mla_w_uv, ev_w_out, od_norm_w, od_w_in, mlstm_conv_w, mlstm_conv_b, mlstm_w_q, mlstm_w_k, mlstm_w_v, mlstm_w_gate, mlstm_b_gate, mlstm_norm_w, mlstm_skip, od_w_out, final_norm_w):
    B, seq, Dm = x_prompt.shape
    Bd, Ls, _ = x_sample.shape
    NE, NO = ev_norm_w.shape[0], od_norm_w.shape[0]
    depth = NE + NO
    T = seq + N_META
    Tp = -(-T // ROW_ALIGN) * ROW_ALIGN
    pad = Tp - T
    assert seq % ROW_BLOCK == 0 and (pad + N_META) % ROW_BLOCK == 0 and (Bd * Ls) % ROW_BLOCK == 0
    assert Ls >= CONV_W - 1
    past_len = page_table.shape[1] * cache_mla_ckv.shape[2]
    scale = (MLA_NOPE + MLA_ROPE) ** -0.5
    cache_krt = jnp.swapaxes(cache_mla_krope, 2, 3)
    mlstm_state = (state_mlstm_C, state_mlstm_n[:, :, :, None, :],
                   jnp.broadcast_to(state_mlstm_m[:, :, :, None, None], state_mlstm_m.shape + (1, LANES)))

    hp = jnp.concatenate([jnp.zeros((B, pad, Dm), F32), jnp.broadcast_to(meta_tokens[None], (B, N_META, Dm)), x_prompt],
                         axis=1).reshape(B * Tp, Dm)
    hs = x_sample.reshape(Bd * Ls, Dm)
    cos_p, sin_p = _rope_tables(jnp.arange(Tp) - pad)
    bias_lane = (jnp.arange(LANES) == MLA_ROPE).astype(F32)[None, :]
    kbias_p = jnp.where(jnp.arange(Tp) < pad, NEG_BIG, 0.0).astype(F32)[:, None] * bias_lane
    tables_p = tuple(jnp.tile(t, (B, 1)) for t in (cos_p, sin_p, kbias_p))
    cos_s, sin_s = _rope_tables(past_len + jnp.arange(Ls))
    tables_s = (jnp.tile(cos_s, (Bd, 1)), jnp.tile(sin_s, (Bd, 1)), jnp.zeros((Bd * Ls, LANES), F32))
    sm = jax.nn.softmax(hgrn_lower_bounds.astype(F32), axis=0)
    lbs = jnp.cumsum(sm, axis=0) - sm[0]

    def attend_prompt(Q, K, KT):
        return attn_prompt(Q, K, KT, B, Tp, scale)

    def make_attend_sample(e):
        def attend(Q, K, KT):
            Qs = Q.reshape(MLA_HEADS, Bd, Ls, QK_WIDTH).transpose(1, 0, 2, 3).reshape(Bd, MLA_HEADS * Ls, QK_WIDTH)
            o = attn_decode(Qs, K.reshape(Bd, Ls, QK_WIDTH), cache_mla_ckv, cache_krt, page_table, e, scale)
            return o.reshape(Bd, MLA_HEADS, Ls, KV_LORA).transpose(1, 0, 2, 3).reshape(MLA_HEADS, Bd * Ls, KV_LORA)
        return attend

    outs_p = {k: [] for k in ("ckv", "kr", "n", "m", "cv")}
    outs_s = {k: [] for k in ("ckv", "kr", "n", "m", "cv")}
    S_p = S_s = C_p = C_s = None
    for l in range(depth):
        e = l // 2
        if l % 2 == 0:
            p = _prep_even(e, ev_norm_w, ev_w_in, hgrn_norm_w, mla_q_norm_w, mla_w_q_b, mla_kv_norm_w, mla_w_uk,
                           mla_w_uv, ev_w_out)
            lb = lbs[e][None, :]
            hp, c, kr, S_p = _even_layer(hp, p, lb, tables_p, B, Tp, HGRN_CHUNK, None, S_p, e, NE, attend_prompt)
            outs_p["ckv"].append(c.reshape(B, Tp, KV_LORA)[:, pad:])
            outs_p["kr"].append(kr.reshape(B, Tp, MLA_ROPE)[:, pad:])
            hs, c, kr, S_s = _even_layer(hs, p, lb, tables_s, Bd, Ls, Ls, state_hgrn, S_s, e, NE,
                                         make_attend_sample(e))
            outs_s["ckv"].append(c.reshape(Bd, Ls, KV_LORA))
            outs_s["kr"].append(kr.reshape(Bd, Ls, MLA_ROPE))
        else:
            p = _prep_odd(e, od_norm_w, od_w_in, mlstm_conv_w, mlstm_conv_b, mlstm_w_q, mlstm_w_k, mlstm_w_v,
                          mlstm_w_gate, mlstm_b_gate, mlstm_norm_w, mlstm_skip, od_w_out)
            hp, C_p, n, m, cv = _odd_layer(hp, p, B, Tp, MLSTM_CHUNK, pad, None, C_p, e, NO, None)
            for key, val in zip(("n", "m", "cv"), (n, m, cv)):
                outs_p[key].append(val)
            hs, C_s, n, m, cv = _odd_layer(hs, p, Bd, Ls, Ls, 0, mlstm_state, C_s, e, NO, state_mlstm_conv[e])
            for key, val in zip(("n", "m", "cv"), (n, m, cv)):
                outs_s[key].append(val)

    fw = final_norm_w[None, :]
    blocks_in = Tp // ROW_BLOCK
    skip_blocks = (pad + N_META) // ROW_BLOCK
    y_prompt = final_norm(hp, fw, skip_blocks, blocks_in, blocks_in - skip_blocks).reshape(B, seq, Dm)
    y_sample = final_norm(hs, fw, 0, (Bd * Ls) // ROW_BLOCK, (Bd * Ls) // ROW_BLOCK).reshape(Bd, Ls, Dm)
    st = lambda xs: jnp.stack(xs)
    return (y_prompt, y_sample,
            st(outs_p["ckv"]), st(outs_p["kr"]), S_p, C_p, st(outs_p["n"]), st(outs_p["m"]), st(outs_p["cv"]),
            st(outs_s["ckv"]), st(outs_s["kr"]), S_s, C_s, st(outs_s["n"]), st(outs_s["m"]), st(outs_s["cv"]))
```

```python
import functools

import jax
import jax.numpy as jnp
from jax import lax
from jax.experimental import pallas as pl
from jax.experimental.pallas import tpu as pltpu

F32 = jnp.float32
BF16 = jnp.bfloat16

N_META = 16
HGRN_HEADS = 8
HGRN_DK = 128
MLA_HEADS = 8
MLA_NOPE = 128
MLA_ROPE = 64
MLA_V = 128
Q_LORA = 512
KV_LORA = 256
ROPE_THETA = 10000.0
MLSTM_HEADS = 4
QKV_BLOCK = 4
CONV_W = 4
NEG_BIG = -1e30
RMS_EPS = 1e-6
LN_EPS = 1e-5
LOG2E = 1.4426950408889634

LANES = 128
ROW_ALIGN = 256
ROW_BLOCK = 256
QK_WIDTH = KV_LORA + LANES
Q_BLOCK = 128
KV_BLOCK = 512
ATTN_ROW_SPLITS = 1
HGRN_CHUNK = 128
MLSTM_CHUNK = 256
SUB = 32
PAGES_PER_STEP = 16
HGRN_SAFE_EXP = 60.0
BD_CHUNK = 256
VMEM_LIMIT = 56 * 1024 * 1024

_NT = (((1,), (1,)), ((), ()))
_TN = (((0,), (0,)), ((), ()))


def _dot(a, b):
    return jnp.dot(a, b, preferred_element_type=F32)


def _dot_nt(a, b):
    return lax.dot_general(a, b, _NT, preferred_element_type=F32)


def _dot_tn(a, b):
    return lax.dot_general(a, b, _TN, preferred_element_type=F32)


def _params(sem):
    return pltpu.CompilerParams(dimension_semantics=sem, vmem_limit_bytes=VMEM_LIMIT)


def _rms(x, w, eps=RMS_EPS):
    return x * lax.rsqrt(jnp.mean(x * x, axis=-1, keepdims=True) + eps) * w


def _silu(x):
    return x * jax.nn.sigmoid(x)


def _const_spec(shape):
    nd = len(shape)
    return pl.BlockSpec(shape, lambda *_: (0,) * nd, pipeline_mode=pl.Buffered(1))


def _norm_matmul_kernel(x_ref, nw_ref, w_ref, o_ref, *, chunks):
    xn = _rms(x_ref[...], nw_ref[...]).astype(BF16)
    for start, width in chunks:
        o_ref[:, start:start + width] = _dot(xn, w_ref[:, start:start + width])


def norm_matmul(x, nw, w):
    R, D = x.shape
    N = w.shape[1]
    tm = ROW_BLOCK
    chunks, s = [], 0
    while s < N:
        width = min(1024, N - s)
        chunks.append((s, width))
        s += width
    return pl.pallas_call(
        functools.partial(_norm_matmul_kernel, chunks=tuple(chunks)),
        out_shape=jax.ShapeDtypeStruct((R, N), F32),
        grid=(R // tm,),
        in_specs=[pl.BlockSpec((tm, D), lambda i: (i, 0)), _const_spec((1, D)), _const_spec((D, N))],
        out_specs=pl.BlockSpec((tm, N), lambda i: (i, 0)),
        compiler_params=_params(("parallel",)),
        name="norm_matmul",
    )(x, nw, w)


def _rope_tile(x, cos, sin):
    lane = lax.broadcasted_iota(jnp.int32, x.shape, 1)
    half = MLA_ROPE // 2
    partner = jnp.where(lane % MLA_ROPE < half, pltpu.roll(x, LANES - half, 1), pltpu.roll(x, half, 1))
    return x * cos + partner * sin


def _mla_prep_kernel(cq_ref, ckv_ref, kpe_ref, cos_ref, sin_ref, kbias_ref, qnw_ref, wqb_ref, wuk_ref, kvnw_ref,
                     q_ref, k_ref, kt_ref, c_ref, kr_ref):
    cos = cos_ref[...]
    sin = sin_ref[...]
    lane = lax.broadcasted_iota(jnp.int32, (1, LANES), 1)
    q_one = jnp.where(lane == MLA_ROPE, 1.0, 0.0)
    cqn = _rms(cq_ref[...], qnw_ref[...]).astype(BF16)
    qb = _dot(cqn, wqb_ref[...])
    nope_w = MLA_HEADS * MLA_NOPE
    for h in range(MLA_HEADS):
        qn = qb[:, h * MLA_NOPE:(h + 1) * MLA_NOPE].astype(BF16)
        q_ref[h, :, 0:KV_LORA] = _dot(qn, wuk_ref[h]).astype(BF16)
        qpe = qb[:, nope_w + h * LANES: nope_w + (h + 1) * LANES]
        q_ref[h, :, KV_LORA:QK_WIDTH] = (_rope_tile(qpe, cos, sin) + q_one).astype(BF16)
    c = _rms(ckv_ref[...], kvnw_ref[...])
    kr = _rope_tile(kpe_ref[...], cos, sin)
    c_ref[...] = c
    kr_ref[...] = kr[:, 0:MLA_ROPE]
    krb = kr + kbias_ref[...]
    k_ref[:, 0:KV_LORA] = c.astype(BF16)
    k_ref[:, KV_LORA:QK_WIDTH] = krb.astype(BF16)
    kt_ref[0:KV_LORA, :] = c.T.astype(BF16)
    kt_ref[KV_LORA:QK_WIDTH, :] = krb.T.astype(BF16)


def mla_prep(u, cos, sin, kbias, qnw, wqb, wuk_t, kvnw, col0):
    R = u.shape[0]
    tm = ROW_BLOCK
    cq_blk = col0 // Q_LORA
    ckv_blk = (col0 + Q_LORA) // KV_LORA
    kpe_blk = (col0 + Q_LORA + KV_LORA) // LANES
    return pl.pallas_call(
        _mla_prep_kernel,
        out_shape=(jax.ShapeDtypeStruct((MLA_HEADS, R, QK_WIDTH), BF16),
                   jax.ShapeDtypeStruct((R, QK_WIDTH), BF16),
                   jax.ShapeDtypeStruct((QK_WIDTH, R), BF16),
                   jax.ShapeDtypeStruct((R, KV_LORA), F32),
                   jax.ShapeDtypeStruct((R, MLA_ROPE), F32)),
        grid=(R // tm,),
        in_specs=[pl.BlockSpec((tm, Q_LORA), lambda i: (i, cq_blk)),
                  pl.BlockSpec((tm, KV_LORA), lambda i: (i, ckv_blk)),
                  pl.BlockSpec((tm, LANES), lambda i: (i, kpe_blk)),
                  pl.BlockSpec((tm, LANES), lambda i: (i, 0)),
                  pl.BlockSpec((tm, LANES), lambda i: (i, 0)),
                  pl.BlockSpec((tm, LANES), lambda i: (i, 0)),
                  _const_spec(qnw.shape), _const_spec(wqb.shape), _const_spec(wuk_t.shape), _const_spec(kvnw.shape)],
        out_specs=(pl.BlockSpec((MLA_HEADS, tm, QK_WIDTH), lambda i: (0, i, 0)),
                   pl.BlockSpec((tm, QK_WIDTH), lambda i: (i, 0)),
                   pl.BlockSpec((QK_WIDTH, tm), lambda i: (0, i)),
                   pl.BlockSpec((tm, KV_LORA), lambda i: (i, 0)),
                   pl.BlockSpec((tm, MLA_ROPE), lambda i: (i, 0))),
        compiler_params=_params(("parallel",)),
        name="mla_prep",
    )(u, u, u, cos, sin, kbias, qnw, wqb, wuk_t, kvnw)


def _even_out_kernel(x_ref, oa_ref, ga_ref, gb_ref, ob_ref, hnw_ref, wuv_ref, wout_ref, o_ref, y_scr):
    hw = HGRN_HEADS * HGRN_DK
    for h in range(HGRN_HEADS):
        sl = slice(h * HGRN_DK, (h + 1) * HGRN_DK)
        ya = _rms(oa_ref[:, sl], hnw_ref[:, sl])
        y_scr[:, sl] = (ya * _silu(ga_ref[:, sl])).astype(BF16)
    for h in range(MLA_HEADS):
        sl = slice(h * MLA_V, (h + 1) * MLA_V)
        yb = _dot(ob_ref[h], wuv_ref[h])
        y_scr[:, hw + h * MLA_V: hw + (h + 1) * MLA_V] = (yb * _silu(gb_ref[:, sl])).astype(BF16)
    o_ref[...] = x_ref[...] + _dot(y_scr[...], wout_ref[...])


def even_out(x, o_a, u, o_b, hnw, wuv, wout, ga_blk, gb_blk):
    R, D = x.shape
    tm = ROW_BLOCK
    W = HGRN_HEADS * HGRN_DK
    return pl.pallas_call(
        _even_out_kernel,
        out_shape=jax.ShapeDtypeStruct((R, D), F32),
        grid=(R // tm,),
        in_specs=[pl.BlockSpec((tm, D), lambda i: (i, 0)),
                  pl.BlockSpec((tm, W), lambda i: (i, 0)),
                  pl.BlockSpec((tm, W), lambda i: (i, ga_blk)),
                  pl.BlockSpec((tm, W), lambda i: (i, gb_blk)),
                  pl.BlockSpec((MLA_HEADS, tm, KV_LORA), lambda i: (0, i, 0)),
                  _const_spec(hnw.shape), _const_spec(wuv.shape), _const_spec(wout.shape)],
        out_specs=pl.BlockSpec((tm, D), lambda i: (i, 0)),
        scratch_shapes=[pltpu.VMEM((tm, 2 * W), BF16)],
        compiler_params=_params(("parallel",)),
        name="even_out",
    )(x, o_a, u, u, o_b, hnw, wuv, wout)


def _odd_mid_kernel(*refs, tm, blocks_per_seq, shifted_inputs, k_scale):
    if shifted_inputs:
        x_ref, x1_ref, x2_ref, x3_ref = refs[:4]
        rest = refs[4:]
    else:
        x_ref, halo_ref = refs[:2]
        rest = refs[2:]
    cw_ref, cb_ref, wq_ref, wk_ref, wv_ref, wg_ref, bg_ref, xc_ref, q_ref, k_ref, v_ref, g_ref = rest[:12]
    x = x_ref[...]
    if shifted_inputs:
        shifted = [x1_ref[...], x2_ref[...], x3_ref[...]]
    else:
        xs = rest[12]
        first = (pl.program_id(0) % blocks_per_seq) == 0
        halo = halo_ref[...]
        xs[0:8, :] = jnp.where(first, jnp.zeros_like(halo), halo)
        xs[8:, :] = x
        shifted = [xs[pl.ds(8 - s, tm), :] for s in (1, 2, 3)]
    cw = cw_ref[...]
    pre = cb_ref[...] + x * cw[3:4] + shifted[0] * cw[2:3] + shifted[1] * cw[1:2] + shifted[2] * cw[0:1]
    xc = _silu(pre)
    xc_ref[...] = xc
    D = x.shape[1]
    g = jnp.zeros((x.shape[0], LANES), F32) + bg_ref[...]
    for c in range(D // BD_CHUNK):
        sl = slice(c * BD_CHUNK, (c + 1) * BD_CHUNK)
        xcb = xc[:, sl].astype(BF16)
        q = _dot(xcb, wq_ref[c])
        k = _dot(xcb, wk_ref[c])
        v = _dot(x[:, sl].astype(BF16), wv_ref[c])
        q_ref[:, sl] = q.astype(BF16)
        k_ref[:, sl] = (k * k_scale).astype(BF16)
        v_ref[:, sl] = v.astype(BF16)
        g = g + _dot(q.astype(BF16), wg_ref[c * BD_CHUNK:(c + 1) * BD_CHUNK, :])
        g = g + _dot(k.astype(BF16), wg_ref[D + c * BD_CHUNK: D + (c + 1) * BD_CHUNK, :])
        g = g + _dot(v.astype(BF16), wg_ref[2 * D + c * BD_CHUNK: 2 * D + (c + 1) * BD_CHUNK, :])
    g_ref[...] = g


def odd_mid(u, shifts, rows_per_seq, cw, cb, wq, wk, wv, wg, bg, k_scale):
    R = u.shape[0]
    D = u.shape[1] // 2
    tm = ROW_BLOCK
    row = lambda i: (i, 0)
    wspecs = [_const_spec(a.shape) for a in (cw, cb, wq, wk, wv, wg, bg)]
    outs = ((jax.ShapeDtypeStruct((R, D), F32),) + tuple(jax.ShapeDtypeStruct((R, D), BF16) for _ in range(3))
            + (jax.ShapeDtypeStruct((R, LANES), F32),))
    out_specs = tuple(pl.BlockSpec((tm, D), row) for _ in range(4)) + (pl.BlockSpec((tm, LANES), row),)
    if shifts is None:
        ins = [pl.BlockSpec((tm, D), row),
               pl.BlockSpec((8, D), lambda i: (jnp.maximum(i * (tm // 8) - 1, 0), 0))]
        args = (u, u)
        scratch = [pltpu.VMEM((tm + 8, D), F32)]
    else:
        ins = [pl.BlockSpec((tm, D), row)] * 4
        args = (u,) + tuple(shifts)
        scratch = []
    kern = functools.partial(_odd_mid_kernel, tm=tm, blocks_per_seq=rows_per_seq // tm,
                             shifted_inputs=shifts is not None, k_scale=k_scale)
    return pl.pallas_call(
        kern, out_shape=outs, grid=(R // tm,), in_specs=ins + wspecs, out_specs=out_specs,
        scratch_shapes=scratch, compiler_params=_params(("parallel",)), name="odd_mid",
    )(*args, cw, cb, wq, wk, wv, wg, bg)


def _odd_out_kernel(x_ref, h_ref, xc_ref, z_ref, nw_ref, skip_ref, wout_ref, o_ref, y_scr, *, dh):
    D = h_ref.shape[1]
    for hd in range(D // dh):
        sl = slice(hd * dh, (hd + 1) * dh)
        h = h_ref[:, sl]
        mu = jnp.mean(h, axis=-1, keepdims=True)
        hc = h - mu
        var = jnp.mean(hc * hc, axis=-1, keepdims=True)
        hn = hc * lax.rsqrt(var + LN_EPS) * nw_ref[:, sl]
        y = (hn + skip_ref[:, sl] * xc_ref[:, sl]) * _silu(z_ref[:, sl])
        y_scr[:, sl] = y.astype(BF16)
    o_ref[...] = x_ref[...] + _dot(y_scr[...], wout_ref[...])


def odd_out(x, h, xc, u, nw, skip, wout):
    R, Dm = x.shape
    D = h.shape[1]
    tm = ROW_BLOCK
    row = lambda i: (i, 0)
    return pl.pallas_call(
        functools.partial(_odd_out_kernel, dh=D // MLSTM_HEADS),
        out_shape=jax.ShapeDtypeStruct((R, Dm), F32),
        grid=(R // tm,),
        in_specs=[pl.BlockSpec((tm, Dm), row), pl.BlockSpec((tm, D), row), pl.BlockSpec((tm, D), row),
                  pl.BlockSpec((tm, D), lambda i: (i, 1)),
                  _const_spec(nw.shape), _const_spec(skip.shape), _const_spec(wout.shape)],
        out_specs=pl.BlockSpec((tm, Dm), row),
        scratch_shapes=[pltpu.VMEM((tm, D), BF16)],
        compiler_params=_params(("parallel",)),
        name="odd_out",
    )(x, h, xc, u, nw, skip, wout)


def _final_norm_kernel(x_ref, w_ref, o_ref):
    o_ref[...] = _rms(x_ref[...], w_ref[...])


def final_norm(x, w, skip_blocks, blocks_per_seq_in, blocks_per_seq_out):
    R, D = x.shape
    tm = ROW_BLOCK
    n_seq = R // (blocks_per_seq_in * tm)
    Ro = n_seq * blocks_per_seq_out * tm

    def in_map(i):
        return ((i // blocks_per_seq_out) * blocks_per_seq_in + skip_blocks + i % blocks_per_seq_out, 0)

    return pl.pallas_call(
        _final_norm_kernel,
        out_shape=jax.ShapeDtypeStruct((Ro, D), F32),
        grid=(Ro // tm,),
        in_specs=[pl.BlockSpec((tm, D), in_map), _const_spec(w.shape)],
        out_specs=pl.BlockSpec((tm, D), lambda i: (i, 0)),
        compiler_params=_params(("parallel",)),
        name="final_norm",
    )(x, w)


def _split3(x):
    hi = x.astype(BF16)
    r1 = x - hi.astype(F32)
    mid = r1.astype(BF16)
    lo = (r1 - mid.astype(F32)).astype(BF16)
    return hi, mid, lo


def _cumsum_rows(x):
    L = x.shape[0]
    r = lax.broadcasted_iota(jnp.int32, (L, L), 0)
    c = lax.broadcasted_iota(jnp.int32, (L, L), 1)
    tri = (c <= r).astype(BF16)
    hi, mid, lo = _split3(x)
    return _dot(tri, hi) + _dot(tri, mid) + _dot(tri, lo)


def _row_to_col(row):
    n = row.shape[1]
    r = lax.broadcasted_iota(jnp.int32, (n, n), 0)
    c = lax.broadcasted_iota(jnp.int32, (n, n), 1)
    return jnp.sum(jnp.where(r == c, row, 0.0), axis=1, keepdims=True)


def _hgrn_chunk_kernel(*refs, Lc, sub, has_state, has_alias):
    qa_ref, fa_ref, ia_ref, lb_ref = refs[:4]
    pos = 4
    s0_ref = None
    if has_state:
        s0_ref = refs[pos]
        pos += 1
    if has_alias:
        pos += 1
    o_ref, sout_ref, s_scr = refs[pos:pos + 3]
    H, K = HGRN_HEADS, HGRN_DK

    @pl.when(pl.program_id(1) == 0)
    def _():
        s_scr[...] = s0_ref[0] if has_state else jnp.zeros_like(s_scr)

    lb = lb_ref[...]
    qa = qa_ref[0]
    fa = fa_ref[0]
    q = _silu(qa)
    logf = jnp.log(lb + (1.0 - lb) * jax.nn.sigmoid(fa))
    k = (1.0 - lb) * jax.nn.sigmoid(-fa)
    vb = ia_ref[0].astype(BF16)
    G = _cumsum_rows(logf)
    qg = q * jnp.exp(G)
    nblk = Lc // sub
    worst = -G[sub - 1:sub]
    for i in range(1, nblk):
        worst = jnp.maximum(worst, G[i * sub - 1:i * sub] - G[(i + 1) * sub - 1:(i + 1) * sub])
    safe = jnp.max(worst) <= HGRN_SAFE_EXP
    row = lax.broadcasted_iota(jnp.int32, (sub, Lc), 0)
    col = lax.broadcasted_iota(jnp.int32, (sub, Lc), 1)
    tr = lax.broadcasted_iota(jnp.int32, (sub, sub), 0)
    tc = lax.broadcasted_iota(jnp.int32, (sub, sub), 1)

    def inter(h):
        sl = slice(h * K, (h + 1) * K)
        return sl, _dot(qg[:, sl].astype(BF16), s_scr[h].astype(BF16))

    @pl.when(safe)
    def _():
        for h in range(H):
            sl, o = inter(h)
            Gh, qh, kh = G[:, sl], q[:, sl], k[:, sl]
            a_rows = []
            for i in range(nblk):
                lo = i * sub
                r = Gh[lo - 1:lo] if i > 0 else jnp.zeros((1, K), F32)
                qd = (qh[lo:lo + sub] * jnp.exp(Gh[lo:lo + sub] - r)).astype(BF16)
                kd = (kh * jnp.exp(jnp.minimum(r - Gh, HGRN_SAFE_EXP))).astype(BF16)
                a_rows.append(jnp.where(col <= lo + row, _dot_nt(qd, kd), 0.0))
            a = a_rows[0] if nblk == 1 else jnp.concatenate(a_rows, axis=0)
            o_ref[0, :, sl] = o + _dot(a.astype(BF16), vb[:, sl])

    @pl.when(jnp.logical_not(safe))
    def _():
        for h in range(H):
            sl, o_int = inter(h)
            Gh, qh, kh, vh = G[:, sl], q[:, sl], k[:, sl], vb[:, sl]
            for i in range(nblk):
                lo = i * sub
                Gi, qi, ki = Gh[lo:lo + sub], qh[lo:lo + sub], kh[lo:lo + sub]
                e = jnp.exp(jnp.minimum(Gi[:, None, :] - Gi[None, :, :], 0.0))
                ad = jnp.sum(qi[:, None, :] * ki[None, :, :] * e, axis=-1)
                ad = jnp.where(tc <= tr, ad, 0.0)
                o = o_int[lo:lo + sub] + _dot(ad.astype(BF16), vh[lo:lo + sub])
                if i > 0:
                    r = Gh[lo - 1:lo]
                    qd = (qi * jnp.exp(Gi - r)).astype(BF16)
                    kd = (kh[:lo] * jnp.exp(r - Gh[:lo])).astype(BF16)
                    o = o + _dot(_dot_nt(qd, kd).astype(BF16), vh[:lo])
                o_ref[0, lo:lo + sub, sl] = o

    GL = G[Lc - 1:Lc]
    kd_all = (k * jnp.exp(GL - G)).astype(BF16)
    eGL = jnp.exp(GL)
    for h in range(H):
        sl = slice(h * K, (h + 1) * K)
        S_new = _row_to_col(eGL[:, sl]) * s_scr[h] + _dot_tn(kd_all[:, sl], vb[:, sl])
        s_scr[h] = S_new
        sout_ref[0, h] = S_new


def hgrn_scan(u3, lb, Lc, state, prev_stack, layer, n_layers):
    B, T, _ = u3.shape
    H, K = HGRN_HEADS, HGRN_DK
    W = H * K
    nc = T // Lc
    sub = min(SUB, Lc)
    st_spec = pl.BlockSpec((None, 1, H, K, K), lambda b, c: (layer, b, 0, 0, 0))
    in_specs = [pl.BlockSpec((1, Lc, W), lambda b, c: (b, c, 0)),
                pl.BlockSpec((1, Lc, W), lambda b, c: (b, c, 1)),
                pl.BlockSpec((1, Lc, W), lambda b, c: (b, c, 2)),
                pl.BlockSpec((1, W), lambda b, c: (0, 0))]
    args = [u3, u3, u3, lb]
    if state is not None:
        in_specs.append(st_spec)
        args.append(state)
    aliases = {}
    if prev_stack is not None:
        aliases = {len(args): 1}
        in_specs.append(pl.BlockSpec(memory_space=pl.ANY))
        args.append(prev_stack)
    kern = functools.partial(_hgrn_chunk_kernel, Lc=Lc, sub=sub, has_state=state is not None,
                             has_alias=prev_stack is not None)
    return pl.pallas_call(
        kern,
        out_shape=(jax.ShapeDtypeStruct((B, T, W), F32), jax.ShapeDtypeStruct((n_layers, B, H, K, K), F32)),
        grid=(B, nc),
        in_specs=in_specs,
        out_specs=(pl.BlockSpec((1, Lc, W), lambda b, c: (b, c, 0)), st_spec),
        scratch_shapes=[pltpu.VMEM((H, K, K), F32)],
        input_output_aliases=aliases,
        compiler_params=_params(("parallel", "arbitrary")),
        name="hgrn_scan",
    )(*args)


def _mlstm_chunk_kernel(*refs, Lc, pad, has_state, has_alias, single_chunk):
    q_ref, k_ref, v_ref, g_ref = refs[:4]
    pos = 4
    if has_state:
        c0_ref, n0_ref, m0_ref = refs[pos:pos + 3]
        pos += 3
    if has_alias:
        pos += 1
    h_ref, cout_ref, nout_ref, mout_ref = refs[pos:pos + 4]
    ci = pl.program_id(1)
    H = MLSTM_HEADS
    dh = q_ref.shape[2] // H
    carried = not single_chunk

    if carried:
        @pl.when(ci == 0)
        def _():
            if has_state:
                cout_ref[...] = c0_ref[...]
                nout_ref[...] = n0_ref[...]
                mout_ref[...] = m0_ref[...]
            else:
                cout_ref[...] = jnp.zeros_like(cout_ref)
                nout_ref[...] = jnp.zeros_like(nout_ref)
                mout_ref[...] = jnp.zeros_like(mout_ref)

    g = g_ref[0]
    t = ci * Lc + lax.broadcasted_iota(jnp.int32, (Lc, 1), 0)
    valid = t >= pad
    r = lax.broadcasted_iota(jnp.int32, (Lc, Lc), 0)
    c = lax.broadcasted_iota(jnp.int32, (Lc, Lc), 1)
    eye = r == c
    tril = c <= r
    for hd in range(H):
        sl = slice(hd * dh, (hd + 1) * dh)
        qb = q_ref[0, :, sl]
        kb = k_ref[0, :, sl]
        vb = v_ref[0, :, sl]
        q = qb.astype(F32)
        ks = kb.astype(F32)
        logi = g[:, hd:hd + 1]
        fpre = g[:, H + hd:H + hd + 1]
        logf = jnp.minimum(fpre, 0.0) - jnp.log1p(jnp.exp(-jnp.abs(fpre)))
        logf = jnp.where(valid, logf, 0.0)
        logi = jnp.where(valid, logi, NEG_BIG)
        logf_row = jnp.sum(jnp.where(eye, logf, 0.0), axis=0, keepdims=True)
        logi_row = jnp.sum(jnp.where(eye, logi, 0.0), axis=0, keepdims=True)
        b_col = jnp.sum(jnp.where(tril, logf_row, 0.0), axis=1, keepdims=True)
        b_row = jnp.sum(jnp.where(c >= r, logf, 0.0), axis=0, keepdims=True)
        if carried:
            C, n, m0 = cout_ref[0, hd], nout_ref[0, hd], mout_ref[0, hd][:, 0:1]
        elif has_state:
            C, n, m0 = c0_ref[0, hd], n0_ref[0, hd], m0_ref[0, hd][:, 0:1]
        else:
            C, n, m0 = jnp.zeros((dh, dh), F32), jnp.zeros((1, dh), F32), jnp.zeros((1, 1), F32)
        D = jnp.where(tril, b_col - b_row + logi_row, NEG_BIG)
        inter = b_col + m0
        m_t = jnp.maximum(inter, jnp.max(D, axis=1, keepdims=True))
        w_inter = jnp.exp(inter - m_t)
        s = _dot_nt(qb, kb) * jnp.exp(D - m_t)
        num = _dot(s.astype(BF16), vb) + w_inter * _dot(qb, C.astype(BF16))
        den = jnp.sum(s, axis=1, keepdims=True) + w_inter * jnp.sum(q * n, axis=1, keepdims=True)
        h_ref[0, :, sl] = num / jnp.maximum(jnp.abs(den), jnp.exp(-m_t))
        m_new = m_t[Lc - 1:Lc]
        bL = b_col[Lc - 1:Lc]
        carry = jnp.exp(bL + m0 - m_new)
        wj = jnp.exp(bL - b_col + logi - m_new)
        kw = ks * wj
        C_new = carry * C + _dot_tn(kw.astype(BF16), vb)
        n_new = carry * n + jnp.sum(kw, axis=0, keepdims=True)
        m_b = jnp.zeros((1, LANES), F32) + m_new
        cout_ref[0, hd] = C_new
        nout_ref[0, hd] = n_new
        mout_ref[0, hd] = m_b


def mlstm_scan(q3, k3, v3, g3, Lc, pad, state, prev_stack, layer, n_layers):
    B, T, D = q3.shape
    H = MLSTM_HEADS
    dh = D // H
    nc = T // Lc
    seq = lambda b, c: (b, c, 0)
    st = lambda b, c: (b, 0, 0, 0)
    st_l = lambda b, c: (layer, b, 0, 0, 0)
    in_specs = [pl.BlockSpec((1, Lc, D), seq), pl.BlockSpec((1, Lc, D), seq), pl.BlockSpec((1, Lc, D), seq),
                pl.BlockSpec((1, Lc, LANES), seq)]
    args = [q3, k3, v3, g3]
    if state is not None:
        in_specs += [pl.BlockSpec((None, 1, H, dh, dh), st_l), pl.BlockSpec((None, 1, H, 1, dh), st_l),
                     pl.BlockSpec((None, 1, H, 1, LANES), st_l)]
        args += list(state)
    aliases = {}
    if prev_stack is not None:
        aliases = {len(args): 1}
        in_specs.append(pl.BlockSpec(memory_space=pl.ANY))
        args.append(prev_stack)
    kern = functools.partial(_mlstm_chunk_kernel, Lc=Lc, pad=pad, has_state=state is not None,
                             has_alias=prev_stack is not None, single_chunk=nc == 1)
    return pl.pallas_call(
        kern,
        out_shape=(jax.ShapeDtypeStruct((B, T, D), F32), jax.ShapeDtypeStruct((n_layers, B, H, dh, dh), F32),
                   jax.ShapeDtypeStruct((B, H, 1, dh), F32), jax.ShapeDtypeStruct((B, H, 1, LANES), F32)),
        grid=(B, nc),
        in_specs=in_specs,
        out_specs=(pl.BlockSpec((1, Lc, D), seq), pl.BlockSpec((None, 1, H, dh, dh), st_l),
                   pl.BlockSpec((1, H, 1, dh), st), pl.BlockSpec((1, H, 1, LANES), st)),
        input_output_aliases=aliases,
        compiler_params=_params(("parallel", "arbitrary")),
        name="mlstm_scan",
    )(*args)


def _attn_prompt_kernel(q_ref, k_ref, kt_ref, o_ref, m_scr, l_scr, acc_scr, sa_scr, sb_scr, *, tq, tk, scale):
    i = pl.program_id(1)
    rows = MLA_HEADS * tq
    q = q_ref[...].reshape(rows, QK_WIDTH)
    m_scr[...] = jnp.full((rows, LANES), NEG_BIG, F32)
    l_scr[...] = jnp.zeros((rows, LANES), F32)
    acc_scr[...] = jnp.zeros((rows, KV_LORA), F32)
    last = ((i + 1) * tq - 1) // tk

    rsz = rows // ATTN_ROW_SPLITS

    scale2 = scale * LOG2E

    kloc = lax.broadcasted_iota(jnp.int32, (1, tk), 1)

    def scores(j, dst):
        dst[...] = _dot(q, kt_ref[:, pl.ds(pl.multiple_of(j * tk, tk), tk)]) * scale2

    def softmax_pv(j, src, causal):
        start = pl.multiple_of(j * tk, tk)
        vb = k_ref[pl.ds(start, tk), 0:KV_LORA]
        for r in range(ATTN_ROW_SPLITS):
            rs = slice(r * rsz, (r + 1) * rsz)
            s = src[rs, :]
            if causal:
                qrel = (i * tq - start) + (r * rsz + lax.broadcasted_iota(jnp.int32, (rsz, 1), 0)) % tq
                s = jnp.where(kloc <= qrel, s, NEG_BIG)
            m_old = m_scr[rs, :]
            m_new = jnp.maximum(m_old, jnp.max(s, axis=1, keepdims=True))
            alpha = jnp.exp2(m_old - m_new)
            p = jnp.concatenate([jnp.exp2(s[:, c * LANES:(c + 1) * LANES] - m_new) for c in range(tk // LANES)],
                                axis=1)
            l_scr[rs, :] = alpha * l_scr[rs, :] + jnp.sum(p, axis=1, keepdims=True)
            pv = _dot(p.astype(BF16), vb)
            acc_scr[rs, :] = jnp.concatenate([alpha] * (KV_LORA // LANES), axis=1) * acc_scr[rs, :] + pv
            m_scr[rs, :] = m_new

    scores(0, sa_scr)
    pairs = last // 2

    def body(t, carry):
        j = 2 * t
        scores(j + 1, sb_scr)
        softmax_pv(j, sa_scr, False)
        scores(j + 2, sa_scr)
        softmax_pv(j + 1, sb_scr, False)
        return carry

    lax.fori_loop(0, pairs, body, 0)

    @pl.when(last % 2 == 0)
    def _():
        softmax_pv(last, sa_scr, True)

    @pl.when(last % 2 == 1)
    def _():
        scores(last, sb_scr)
        softmax_pv(last - 1, sa_scr, False)
        softmax_pv(last, sb_scr, True)

    inv = 1.0 / l_scr[...]
    out = acc_scr[...] * jnp.concatenate([inv] * (KV_LORA // LANES), axis=1)
    o_ref[...] = out.reshape(MLA_HEADS, tq, KV_LORA).astype(BF16)


def attn_prompt(Q, K, KT, B, Tp, scale):
    tq, tk = Q_BLOCK, KV_BLOCK
    assert Tp % tq == 0
    nq = Tp // tq
    rows = MLA_HEADS * tq
    Tk = -(-Tp // tk) * tk
    if Tk > Tp:
        zr = jnp.zeros((Tk - Tp, QK_WIDTH), K.dtype)
        zc = jnp.zeros((QK_WIDTH, Tk - Tp), KT.dtype)
        Kp = jnp.concatenate([a for b in range(B) for a in (K[b * Tp:(b + 1) * Tp], zr)], axis=0)
        KTp = jnp.concatenate([a for b in range(B) for a in (KT[:, b * Tp:(b + 1) * Tp], zc)], axis=1)
    else:
        Kp, KTp = K, KT
    kern = functools.partial(_attn_prompt_kernel, tq=tq, tk=tk, scale=scale)
    return pl.pallas_call(
        kern,
        out_shape=jax.ShapeDtypeStruct((MLA_HEADS, B * Tp, KV_LORA), BF16),
        grid=(B, nq),
        in_specs=[pl.BlockSpec((MLA_HEADS, tq, QK_WIDTH), lambda b, i: (0, b * nq + i, 0)),
                  pl.BlockSpec((Tk, QK_WIDTH), lambda b, i: (b, 0), pipeline_mode=pl.Buffered(1)),
                  pl.BlockSpec((QK_WIDTH, Tk), lambda b, i: (0, b), pipeline_mode=pl.Buffered(1))],
        out_specs=pl.BlockSpec((MLA_HEADS, tq, KV_LORA), lambda b, i: (0, b * nq + i, 0)),
        scratch_shapes=[pltpu.VMEM((rows, LANES), F32), pltpu.VMEM((rows, LANES), F32),
                        pltpu.VMEM((rows, KV_LORA), F32), pltpu.VMEM((rows, tk), F32), pltpu.VMEM((rows, tk), F32)],
        compiler_params=_params(("parallel", "arbitrary")),
        name="attn_prompt",
    )(Q, Kp, KTp)


def _attn_decode_kernel(pt_ref, q_ref, kn_ref, ckv_hbm, krt_hbm, o_ref, ckv_buf, krt_buf, sem,
                        *, layer, n_pages, page, group, L, scale):
    b = pl.program_id(0)
    nb = pl.num_programs(0)
    slot = b % 2
    rows = q_ref.shape[1]

    def page_copies(seq, sl, p):
        pid = pt_ref[seq, p]
        return (pltpu.make_async_copy(ckv_hbm.at[layer, pid], ckv_buf.at[sl, pl.ds(p * page, page)], sem.at[sl, 0]),
                pltpu.make_async_copy(krt_hbm.at[layer, pid], krt_buf.at[sl, :, pl.ds(p * page, page)], sem.at[sl, 1]))

    def start_gather(seq, sl):
        def body(p, carry):
            for cp in page_copies(seq, sl, p):
                cp.start()
            return carry
        lax.fori_loop(0, n_pages, body, 0)

    def wait_gather(seq, sl):
        def body(p, carry):
            for cp in page_copies(seq, sl, p):
                cp.wait()
            return carry
        lax.fori_loop(0, n_pages, body, 0)

    @pl.when(b == 0)
    def _():
        start_gather(0, 0)

    @pl.when(b + 1 < nb)
    def _():
        start_gather(b + 1, 1 - slot)

    q = q_ref[0]
    ql = q[:, 0:KV_LORA]
    qr = q[:, KV_LORA:KV_LORA + MLA_ROPE]
    kn = kn_ref[0]
    tpos = lax.broadcasted_iota(jnp.int32, (rows, L), 0) % L
    jpos = lax.broadcasted_iota(jnp.int32, (rows, L), 1)
    s_new = jnp.where(jpos <= tpos, _dot_nt(q, kn) * scale, NEG_BIG)

    wait_gather(b, slot)
    keys = group * page
    kcs, scores = [], []
    for g in range(n_pages // group):
        kc = ckv_buf[slot, g * keys:(g + 1) * keys, :].astype(BF16)
        krt = krt_buf[slot, :, g * keys:(g + 1) * keys].astype(BF16)
        kcs.append(kc)
        scores.append((_dot_nt(ql, kc) + _dot(qr, krt)) * scale)
    m_cur = scores[0]
    for s in scores[1:]:
        m_cur = jnp.maximum(m_cur, s)
    m = jnp.maximum(jnp.max(m_cur, axis=1, keepdims=True), jnp.max(s_new, axis=1, keepdims=True))
    p_new = jnp.exp(s_new - m)
    ps = [jnp.exp(s - m) for s in scores]
    psum = ps[0]
    for p in ps[1:]:
        psum = psum + p
    l = jnp.sum(psum, axis=1, keepdims=True) + jnp.sum(p_new, axis=1, keepdims=True)
    acc = _dot(p_new.astype(BF16), kn[:, 0:KV_LORA])
    for p, kc in zip(ps, kcs):
        acc = acc + _dot(p.astype(BF16), kc)
    o_ref[0] = (acc / l).astype(BF16)


def attn_decode(Qs, Kn, cache_ckv, cache_krt, page_table, layer, scale):
    Bd, rows, _ = Qs.shape
    L = Kn.shape[1]
    n_pages = page_table.shape[1]
    page = cache_ckv.shape[2]
    group = min(PAGES_PER_STEP, n_pages)
    assert n_pages % group == 0
    kern = functools.partial(_attn_decode_kernel, layer=layer, n_pages=n_pages, page=page, group=group, L=L,
                             scale=scale)
    grid_spec = pltpu.PrefetchScalarGridSpec(
        num_scalar_prefetch=1,
        grid=(Bd,),
        in_specs=[pl.BlockSpec((1, rows, QK_WIDTH), lambda b, pt: (b, 0, 0)),
                  pl.BlockSpec((1, L, QK_WIDTH), lambda b, pt: (b, 0, 0)),
                  pl.BlockSpec(memory_space=pl.ANY), pl.BlockSpec(memory_space=pl.ANY)],
        out_specs=pl.BlockSpec((1, rows, KV_LORA), lambda b, pt: (b, 0, 0)),
        scratch_shapes=[pltpu.VMEM((2, n_pages * page, KV_LORA), F32),
                        pltpu.VMEM((2, MLA_ROPE, n_pages * page), F32),
                        pltpu.SemaphoreType.DMA((2, 2))],
    )
    return pl.pallas_call(
        kern,
        out_shape=jax.ShapeDtypeStruct((Bd, rows, KV_LORA), BF16),
        grid_spec=grid_spec,
        compiler_params=_params(("arbitrary",)),
        name="attn_decode",
    )(page_table, Qs, Kn, cache_ckv, cache_krt)


def _rope_tables(pos):
    half = MLA_ROPE // 2
    inv = ROPE_THETA ** (-jnp.arange(half, dtype=F32) / half)
    ang = pos.astype(F32)[:, None] * inv[None, :]
    cos, sin = jnp.cos(ang), jnp.sin(ang)
    z = jnp.zeros((pos.shape[0], LANES - MLA_ROPE), F32)
    return jnp.concatenate([cos, cos, z], axis=1), jnp.concatenate([-sin, sin, z], axis=1)


def _expand_blockdiag(w):
    G = w.shape[0]
    per = BD_CHUNK // QKV_BLOCK
    rows = w.reshape(G // per, BD_CHUNK, QKV_BLOCK)
    tiled = jnp.tile(rows, (1, 1, per))
    idx = jnp.arange(BD_CHUNK) // QKV_BLOCK
    return jnp.where(idx[:, None] == idx[None, :], tiled, 0.0).astype(BF16)


def _prep_even(e, ev_norm_w, ev_w_in, hgrn_norm_w, mla_q_norm_w, mla_w_q_b, mla_kv_norm_w, mla_w_uk, mla_w_uv, ev_w_out):
    W = HGRN_HEADS * HGRN_DK
    w = ev_w_in[e]
    o = 0
    parts = {}
    for name, width in (("q_a", W), ("f_a", W), ("i_a", W), ("g_a", W), ("cq", Q_LORA), ("ckv", KV_LORA),
                        ("kpe", MLA_ROPE), ("g_b", MLA_HEADS * MLA_V)):
        parts[name] = w[:, o:o + width]
        o += width
    zpad = jnp.zeros((w.shape[0], LANES - MLA_ROPE), w.dtype)
    w_in = jnp.concatenate([parts[n] for n in ("q_a", "f_a", "i_a", "g_a", "g_b", "cq", "ckv", "kpe")] + [zpad], axis=1)
    wqb = mla_w_q_b[e].reshape(Q_LORA, MLA_HEADS, MLA_NOPE + MLA_ROPE)
    nope = wqb[:, :, :MLA_NOPE].reshape(Q_LORA, MLA_HEADS * MLA_NOPE)
    pe = jnp.pad(wqb[:, :, MLA_NOPE:], ((0, 0), (0, 0), (0, LANES - MLA_ROPE))).reshape(Q_LORA, MLA_HEADS * LANES)
    return dict(
        nw=ev_norm_w[e][None, :], w_in=w_in.astype(BF16),
        hnw=hgrn_norm_w[e][None, :], qnw=mla_q_norm_w[e][None, :], kvnw=mla_kv_norm_w[e][None, :],
        wqb=jnp.concatenate([nope, pe], axis=1).astype(BF16),
        wuk_t=jnp.transpose(mla_w_uk[e], (1, 2, 0)).astype(BF16),
        wuv=jnp.transpose(mla_w_uv[e], (1, 0, 2)).astype(BF16),
        wout=ev_w_out[e].astype(BF16),
    )


def _prep_odd(e, od_norm_w, od_w_in, conv_w, conv_b, w_q, w_k, w_v, w_gate, b_gate, norm_w, skip, od_w_out):
    ng = w_gate.shape[2]
    return dict(
        nw=od_norm_w[e][None, :], w_in=od_w_in[e].astype(BF16),
        cw=conv_w[e], cb=conv_b[e][None, :],
        wq=_expand_blockdiag(w_q[e]), wk=_expand_blockdiag(w_k[e]), wv=_expand_blockdiag(w_v[e]),
        wg=jnp.pad(w_gate[e], ((0, 0), (0, LANES - ng))).astype(BF16),
        bg=jnp.pad(b_gate[e], (0, LANES - ng))[None, :],
        lnw=norm_w[e][None, :], skip=skip[e][None, :], wout=od_w_out[e].astype(BF16),
    )


def _even_layer(x, p, lb, tables, n_seq, seq_len, Lc, state, prev_stack, layer, n_layers, attend):
    W = HGRN_HEADS * HGRN_DK
    u = norm_matmul(x, p["nw"], p["w_in"])
    Q, K, KT, c, kr = mla_prep(u, *tables, p["qnw"], p["wqb"], p["wuk_t"], p["kvnw"], col0=5 * W)
    o_a, S_stack = hgrn_scan(u.reshape(n_seq, seq_len, u.shape[1]), lb, Lc, state, prev_stack, layer, n_layers)
    o_b = attend(Q, K, KT)
    x = even_out(x, o_a.reshape(x.shape[0], W), u, o_b, p["hnw"], p["wuv"], p["wout"], ga_blk=3, gb_blk=4)
    return x, c, kr, S_stack


def _odd_layer(x, p, n_seq, seq_len, Lc, pad, state, prev_stack, layer, n_layers, conv_buf):
    u = norm_matmul(x, p["nw"], p["w_in"])
    D = u.shape[1] // 2
    dh = D // MLSTM_HEADS
    u3 = u.reshape(n_seq, seq_len, 2 * D)
    if conv_buf is None:
        shifts = None
    else:
        xp = jnp.concatenate([conv_buf, u3[:, :, :D]], axis=1)
        shifts = [xp[:, CONV_W - 1 - s: CONV_W - 1 - s + seq_len].reshape(n_seq * seq_len, D) for s in (1, 2, 3)]
    xc, q, ks, v, g = odd_mid(u, shifts, seq_len, p["cw"], p["cb"], p["wq"], p["wk"], p["wv"], p["wg"], p["bg"],
                              k_scale=dh ** -0.5)
    r3 = lambda a: a.reshape(n_seq, seq_len, a.shape[1])
    h, C_stack, n, m = mlstm_scan(r3(q), r3(ks), r3(v), r3(g), Lc, pad, state, prev_stack, layer, n_layers)
    x = odd_out(x, h.reshape(x.shape[0], D), xc, u, p["lnw"], p["skip"], p["wout"])
    new_buf = u3[:, seq_len - (CONV_W - 1):, :D]
    return x, C_stack, n[:, :, 0, :], m[:, :, 0, 0], new_buf


def kernel(x_prompt, x_sample, cache_mla_ckv, cache_mla_krope, state_hgrn, state_mlstm_C, state_mlstm_n, state_mlstm_m, state_mlstm_conv, page_table, meta_tokens, ev_norm_w, ev_w_in, hgrn_lower_bounds, hgrn_norm_w, mla_q_norm_w, mla_w_q_b, mla_kv_norm_w, mla_w_uk, mla_w_uv, ev_w_out, od_norm_w, od_w_in, mlstm_conv_w, mlstm_conv_b, mlstm_w_q, mlstm_w_k, mlstm_w_v, mlstm_w_gate, mlstm_b_gate, mlstm_norm_w, mlstm_skip, od_w_out, final_norm_w):
    B, seq, Dm = x_prompt.shape
    Bd, Ls, _ = x_sample.shape
    NE, NO = ev_norm_w.shape[0], od_norm_w.shape[0]
    depth = NE + NO
    T = seq + N_META
    Tp = -(-T // ROW_ALIGN) * ROW_ALIGN
    pad = Tp - T
    assert seq % ROW_BLOCK == 0 and (pad + N_META) % ROW_BLOCK == 0 and (Bd * Ls) % ROW_BLOCK == 0
    assert Ls >= CONV_W - 1
    past_len = page_table.shape[1] * cache_mla_ckv.shape[2]
    scale = (MLA_NOPE + MLA_ROPE) ** -0.5
    cache_krt = jnp.swapaxes(cache_mla_krope, 2, 3)
    mlstm_state = (state_mlstm_C, state_mlstm_n[:, :, :, None, :],
                   jnp.broadcast_to(state_mlstm_m[:, :, :, None, None], state_mlstm_m.shape + (1, LANES)))

    hp = jnp.concatenate([jnp.zeros((B, pad, Dm), F32), jnp.broadcast_to(meta_tokens[None], (B, N_META, Dm)), x_prompt],
                         axis=1).reshape(B * Tp, Dm)
    hs = x_sample.reshape(Bd * Ls, Dm)
    cos_p, sin_p = _rope_tables(jnp.arange(Tp) - pad)
    bias_lane = (jnp.arange(LANES) == MLA_ROPE).astype(F32)[None, :]
    kbias_p = jnp.where(jnp.arange(Tp) < pad, NEG_BIG, 0.0).astype(F32)[:, None] * bias_lane
    tables_p = tuple(jnp.tile(t, (B, 1)) for t in (cos_p, sin_p, kbias_p))
    cos_s, sin_s = _rope_tables(past_len + jnp.arange(Ls))
    tables_s = (jnp.tile(cos_s, (Bd, 1)), jnp.tile(sin_s, (Bd, 1)), jnp.zeros((Bd * Ls, LANES), F32))
    sm = jax.nn.softmax(hgrn_lower_bounds.astype(F32), axis=0)
    lbs = jnp.cumsum(sm, axis=0) - sm[0]

    def attend_prompt(Q, K, KT):
        return attn_prompt(Q, K, KT, B, Tp, scale)

    def make_attend_sample(e):
        def attend(Q, K, KT):
            Qs = Q.reshape(MLA_HEADS, Bd, Ls, QK_WIDTH).transpose(1, 0, 2, 3).reshape(Bd, MLA_HEADS * Ls, QK_WIDTH)
            o = attn_decode(Qs, K.reshape(Bd, Ls, QK_WIDTH), cache_mla_ckv, cache_krt, page_table, e, scale)
            return o.reshape(Bd, MLA_HEADS, Ls, KV_LORA).transpose(1, 0, 2, 3).reshape(MLA_HEADS, Bd * Ls, KV_LORA)
        return attend

    outs_p = {k: [] for k in ("ckv", "kr", "n", "m", "cv")}
    outs_s = {k: [] for k in ("ckv", "kr", "n", "m", "cv")}
    S_p = S_s = C_p = C_s = None
    for l in range(depth):
        e = l // 2
        if l % 2 == 0:
            p = _prep_even(e, ev_norm_w, ev_w_in, hgrn_norm_w, mla_q_norm_w, mla_w_q_b, mla_kv_norm_w, mla_w_uk,
                           mla_w_uv, ev_w_out)
            lb = lbs[e][None, :]
            hp, c, kr, S_p = _even_layer(hp, p, lb, tables_p, B, Tp, HGRN_CHUNK, None, S_p, e, NE, attend_prompt)
            outs_p["ckv"].append(c.reshape(B, Tp, KV_LORA)[:, pad:])
            outs_p["kr"].append(kr.reshape(B, Tp, MLA_ROPE)[:, pad:])
            hs, c, kr, S_s = _even_layer(hs, p, lb, tables_s, Bd, Ls, Ls, state_hgrn, S_s, e, NE,
                                         make_attend_sample(e))
            outs_s["ckv"].append(c.reshape(Bd, Ls, KV_LORA))
            outs_s["kr"].append(kr.reshape(Bd, Ls, MLA_ROPE))
        else:
            p = _prep_odd(e, od_norm_w, od_w_in, mlstm_conv_w, mlstm_conv_b, mlstm_w_q, mlstm_w_k, mlstm_w_v,
                          mlstm_w_gate, mlstm_b_gate, mlstm_norm_w, mlstm_skip, od_w_out)
            hp, C_p, n, m, cv = _odd_layer(hp, p, B, Tp, MLSTM_CHUNK, pad, None, C_p, e, NO, None)
            for key, val in zip(("n", "m", "cv"), (n, m, cv)):
                outs_p[key].append(val)
            hs, C_s, n, m, cv = _odd_layer(hs, p, Bd, Ls, Ls, 0, mlstm_state, C_s, e, NO, state_mlstm_conv[e])
            for key, val in zip(("n", "m", "cv"), (n, m, cv)):
                outs_s[key].append(val)

    fw = final_norm_w[None, :]
    blocks_in = Tp // ROW_BLOCK
    skip_blocks = (pad + N_META) // ROW_BLOCK
    y_prompt = final_norm(hp, fw, skip_blocks, blocks_in, blocks_in - skip_blocks).reshape(B, seq, Dm)
    y_sample = final_norm(hs, fw, 0, (Bd * Ls) // ROW_BLOCK, (Bd * Ls) // ROW_BLOCK).reshape(Bd, Ls, Dm)
    st = lambda xs: jnp.stack(xs)
    return (y_prompt, y_sample,
            st(outs_p["ckv"]), st(outs_p["kr"]), S_p, C_p, st(outs_p["n"]), st(outs_p["m"]), st(outs_p["cv"]),
            st(outs_s["ckv"]), st(outs_s["kr"]), S_s, C_s, st(outs_s["n"]), st(outs_s["m"]), st(outs_s["cv"]))
```
